```python
import math
import jax, jax.numpy as jnp
from jax import lax
import numpy as np

D_MODEL = 2048
BATCH = 8
SEQ = 2048
DEPTH = 1

GRID_W = 64
CTX_LEN = 256
EPS = 1e-6
NEG_INF = -1e30
DN_HEADS = 16
DN_DK = 128
DN_DV = 128
DN_W = DN_HEADS * DN_DV
DN_QKV_W = 2 * DN_HEADS * DN_DK + DN_W
CONV_K = 5
DN_CHUNK = 64
NA_HEADS = 16
NA_DH = 128
NA_W = NA_HEADS * NA_DH
NA_KH_MAX = 8
NA_KW = 16
ROPE_THETA = 10000.0
N_EXPERTS = 16
EC_CAPACITY_FACTOR = 2
EXPERT_FF = 1024
IN_SIZES = (DN_HEADS * DN_DK, DN_HEADS * DN_DK, DN_W, DN_W, 2 * DN_HEADS, 2 * DN_HEADS, NA_W, NA_W, NA_W, D_MODEL, D_MODEL)
IN_W = sum(IN_SIZES)

kernel_name = "hybrid_deltanet_natten_ecmoe_dit_block"


def rmsnorm(x, w):
    xf = x.astype(jnp.float32)
    y = xf * lax.rsqrt(jnp.mean(xf * xf, axis=-1, keepdims=True) + EPS)
    return (y * w.astype(jnp.float32)).astype(x.dtype)


def l2norm(x):
    xf = x.astype(jnp.float32)
    return xf * lax.rsqrt(jnp.sum(xf * xf, axis=-1, keepdims=True) + EPS)


def short_conv(u, w):
    y = lax.conv_general_dilated(u, w[:, None, :].astype(u.dtype), window_strides=(1,),
                                 padding=[(CONV_K // 2, CONV_K // 2)],
                                 dimension_numbers=('NWC', 'WIO', 'NWC'),
                                 feature_group_count=u.shape[-1])
    return jax.nn.silu(y)


def gated_delta_chunked(q, k, v, beta, g, s0):
    q, k, v, beta, g, s0 = [t.astype(jnp.float32) for t in (q, k, v, beta, g, s0)]
    B, H, L, dk = q.shape
    dv = v.shape[-1]
    n = L // DN_CHUNK
    q, k, v = [t.reshape(B, H, n, DN_CHUNK, t.shape[-1]) for t in (q, k, v)]
    beta, g = [t.reshape(B, H, n, DN_CHUNK) for t in (beta, g)]
    G = jnp.cumsum(g, axis=-1)
    idx = jnp.arange(DN_CHUNK)
    tril = idx[:, None] >= idx[None, :]
    strict = idx[:, None] > idx[None, :]
    decay = jnp.exp(jnp.where(tril, G[..., :, None] - G[..., None, :], -jnp.inf))
    kb = k * beta[..., None]
    lower = jnp.where(strict, jnp.einsum('bhncd,bhnsd->bhncs', kb, k) * decay, 0.0)
    rhs = jnp.concatenate([v * beta[..., None], kb * jnp.exp(G)[..., None]], axis=-1)
    sol = lax.linalg.triangular_solve(jnp.eye(DN_CHUNK, dtype=jnp.float32) + lower, rhs,
                                      left_side=True, lower=True, unit_diagonal=True)
    u, w = sol[..., :dv], sol[..., dv:]
    a_qk = jnp.where(tril, jnp.einsum('bhncd,bhnsd->bhncs', q, k) * decay, 0.0)
    g_last = G[..., -1]
    k_dec = k * jnp.exp(g_last[..., None] - G)[..., None]
    q_dec = q * jnp.exp(G)[..., None]

    def step(S, xs):
        q_i, k_i, u_i, w_i, a_i, gl_i = xs
        v_new = u_i - jnp.einsum('bhck,bhkv->bhcv', w_i, S)
        o_i = jnp.einsum('bhck,bhkv->bhcv', q_i, S) + jnp.einsum('bhcs,bhsv->bhcv', a_i, v_new)
        S = S * jnp.exp(gl_i)[..., None, None] + jnp.einsum('bhck,bhcv->bhkv', k_i, v_new)
        return S, o_i

    xs = tuple(jnp.moveaxis(t, 2, 0) for t in (q_dec, k_dec, u, w, a_qk, g_last))
    s_fin, o = lax.scan(step, s0, xs)
    return jnp.moveaxis(o, 0, 2).reshape(B, H, L, dv), s_fin


def delta_inputs(pq, pk, pv, pbeta, pa, conv_w, a_log, dt_bias):
    B, L, _ = pq.shape
    qkv = short_conv(jnp.concatenate([pq, pk, pv], axis=-1), conv_w)
    q, k, v = jnp.split(qkv, [DN_HEADS * DN_DK, 2 * DN_HEADS * DN_DK], axis=-1)
    q = l2norm(q.reshape(B, L, DN_HEADS, DN_DK)).transpose(0, 2, 1, 3) * DN_DK ** -0.5
    k = l2norm(k.reshape(B, L, DN_HEADS, DN_DK)).transpose(0, 2, 1, 3)
    v = v.reshape(B, L, DN_HEADS, DN_DV).transpose(0, 2, 1, 3)
    beta = jax.nn.sigmoid(pbeta.astype(jnp.float32)).transpose(0, 2, 1)
    g = -jnp.exp(a_log.reshape(-1).astype(jnp.float32))[:, None] * jax.nn.softplus(
        pa.astype(jnp.float32).transpose(0, 2, 1) + dt_bias.reshape(-1).astype(jnp.float32)[:, None])
    return q, k, v, beta, g


def delta_branch(p_lat, p_ctx, z, conv_w, a_log, dt_bias, norm_w):
    ql, kl, vl, bl, gl = delta_inputs(*p_lat, conv_w, a_log, dt_bias)
    qc, kc, vc, bc, gc = delta_inputs(*p_ctx, conv_w, a_log, dt_bias)
    H = DN_HEADS
    B, L, _ = z.shape
    s0 = jnp.zeros((B, H, DN_DK, DN_DV), jnp.float32)
    flip = lambda t: jnp.flip(t, axis=2)
    _, s_ctx_f = gated_delta_chunked(qc, kc, vc, bc[:, :H], gc[:, :H], s0)
    _, s_ctx_b = gated_delta_chunked(flip(qc), flip(kc), flip(vc), flip(bc[:, H:]), flip(gc[:, H:]), s0)
    o_f, _ = gated_delta_chunked(ql, kl, vl, bl[:, :H], gl[:, :H], s_ctx_f)
    o_b, _ = gated_delta_chunked(flip(ql), flip(kl), flip(vl), flip(bl[:, H:]), flip(gl[:, H:]), s_ctx_b)
    o = (o_f + flip(o_b)).transpose(0, 2, 1, 3)
    o = rmsnorm(o, norm_w) * jax.nn.silu(z.reshape(B, L, H, DN_DV).astype(jnp.float32))
    return o.reshape(B, L, DN_W).astype(z.dtype)


def axial_rope_tables(n_tokens):
    pos = jnp.arange(n_tokens)
    row = (pos // GRID_W).astype(jnp.float32)
    col = (pos % GRID_W).astype(jnp.float32)
    half = NA_DH // 2
    inv_freq = ROPE_THETA ** (-jnp.arange(0, half, 2, dtype=jnp.float32) / half)
    ang_r = row[:, None] * inv_freq[None, :]
    ang_c = col[:, None] * inv_freq[None, :]
    return (jnp.cos(ang_r)[:, None, :], jnp.sin(ang_r)[:, None, :],
            jnp.cos(ang_c)[:, None, :], jnp.sin(ang_c)[:, None, :])


def _rotate(x, cos, sin):
    x1, x2 = jnp.split(x, 2, axis=-1)
    return jnp.concatenate([x1 * cos - x2 * sin, x2 * cos + x1 * sin], axis=-1)


def apply_axial_rope(x, tabs):
    cr, sr, cc, sc = tabs
    xr, xc = jnp.split(x.astype(jnp.float32), 2, axis=-1)
    return jnp.concatenate([_rotate(xr, cr, sr), _rotate(xc, cc, sc)], axis=-1)


def neighbourhood_attention(q, k, v, k_ctx, v_ctx, rpb, rows):
    B, H, _, _, dh = q.shape
    kh = min(NA_KH_MAX, rows)
    n_cb = GRID_W // NA_KW
    kcb = 2 * NA_KW
    qcol = np.arange(GRID_W).reshape(n_cb, NA_KW)
    qstart = np.clip(qcol - NA_KW // 2, 0, GRID_W - NA_KW)
    kstart = np.clip(np.arange(n_cb) * NA_KW - NA_KW // 2, 0, GRID_W - kcb)
    kcol = kstart[:, None] + np.arange(kcb)[None, :]
    in_win = (kcol[:, None, :] >= qstart[:, :, None]) & (kcol[:, None, :] < qstart[:, :, None] + NA_KW)
    dc_idx = np.clip(kcol[:, None, :] - qcol[:, :, None] + NA_KW - 1, 0, 2 * NA_KW - 2)
    rpb_c = rpb[:, :, dc_idx].astype(jnp.float32)
    mask = jnp.asarray(in_win)[:, :, None, :]
    n_loc = kh * kcb

    def row_block(r):
        rs = jnp.clip(r - kh // 2, 0, rows - kh)
        kb = lax.dynamic_slice_in_dim(k, rs, kh, axis=2)[:, :, :, kcol]
        vb = lax.dynamic_slice_in_dim(v, rs, kh, axis=2)[:, :, :, kcol]
        qr = lax.dynamic_index_in_dim(q, r, axis=2, keepdims=False).reshape(B, H, n_cb, NA_KW, dh)
        dr_idx = rs + jnp.arange(kh) - r + NA_KH_MAX - 1
        bias = jnp.take(rpb_c, dr_idx, axis=1).transpose(0, 2, 3, 1, 4)
        s_loc = jnp.einsum('bhnqd,bhinkd->bhnqik', qr, kb).astype(jnp.float32) + bias
        s_loc = jnp.where(mask, s_loc, NEG_INF).reshape(B, H, n_cb, NA_KW, n_loc)
        s_ctx = jnp.einsum('bhnqd,bhcd->bhnqc', qr, k_ctx).astype(jnp.float32)
        p = jax.nn.softmax(jnp.concatenate([s_loc, s_ctx], axis=-1), axis=-1).astype(v.dtype)
        p_loc = p[..., :n_loc].reshape(B, H, n_cb, NA_KW, kh, kcb)
        o = (jnp.einsum('bhnqik,bhinkd->bhnqd', p_loc, vb)
             + jnp.einsum('bhnqc,bhcd->bhnqd', p[..., n_loc:], v_ctx))
        return o.reshape(B, H, GRID_W, dh)

    return lax.map(row_block, jnp.arange(rows))


def na_branch(pq, pk, pv, pk_ctx, pv_ctx, q_norm_w, k_norm_w, rpb, tabs):
    B, L, _ = pq.shape
    rows = L // GRID_W
    dtype = pv.dtype
    heads = lambda t: t.reshape(t.shape[0], t.shape[1], NA_HEADS, NA_DH)
    q = apply_axial_rope(rmsnorm(heads(pq), q_norm_w), tabs) * NA_DH ** -0.5
    k = apply_axial_rope(rmsnorm(heads(pk), k_norm_w), tabs)
    grid = lambda t: t.astype(dtype).reshape(B, rows, GRID_W, NA_HEADS, NA_DH).transpose(0, 3, 1, 2, 4)
    q, k, v = grid(q), grid(k), grid(heads(pv))
    k_ctx = rmsnorm(heads(pk_ctx), k_norm_w).transpose(0, 2, 1, 3)
    v_ctx = heads(pv_ctx).transpose(0, 2, 1, 3)
    out = neighbourhood_attention(q, k, v, k_ctx, v_ctx, rpb, rows)
    return out.transpose(1, 0, 3, 2, 4).reshape(B, L, NA_W)


def expert_choice_moe(h, w_router, w1, w3, w2):
    B, L, D = h.shape
    cap = EC_CAPACITY_FACTOR * L // N_EXPERTS
    aff = jax.nn.softmax((h @ w_router).astype(jnp.float32), axis=-1)
    gate, idx = lax.top_k(aff.transpose(0, 2, 1), cap)
    bidx = jnp.arange(B)[:, None, None]
    xg = h[bidx, idx]
    hid = jax.nn.silu(jnp.einsum('becd,edf->becf', xg, w1)) * jnp.einsum('becd,edf->becf', xg, w3)
    ye = jnp.einsum('becf,efd->becd', hid, w2) * gate[..., None].astype(h.dtype)
    return jnp.zeros_like(h).at[bidx, idx].add(ye)


def setup_inputs(seed: int = 0) -> dict:
    key = jax.random.key(seed)
    ks = jax.random.split(key, 24)
    f32 = jnp.float32

    def nrm(k, shape, scale):
        return jax.random.normal(k, shape, f32) * scale

    dt = jnp.exp(jax.random.uniform(ks[9], (DEPTH, 2, DN_HEADS), f32, minval=math.log(1e-3), maxval=math.log(1e-1)))
    return {
        "x": nrm(ks[0], (BATCH, SEQ, D_MODEL), 1.0),
        "c": nrm(ks[1], (BATCH, D_MODEL), 1.0),
        "ctx": nrm(ks[2], (BATCH, CTX_LEN, D_MODEL), 1.0),
        "c_ctx": nrm(ks[3], (D_MODEL,), 1.0),
        "ada_w": nrm(ks[4], (DEPTH, D_MODEL, 6 * D_MODEL), D_MODEL ** -0.5),
        "ada_b": nrm(ks[5], (DEPTH, 6 * D_MODEL), 0.01),
        "norm1_w": 1.0 + nrm(ks[6], (DEPTH, D_MODEL), 0.02),
        "w_in": nrm(ks[7], (DEPTH, D_MODEL, IN_W), D_MODEL ** -0.5),
        "conv_w": nrm(ks[8], (DEPTH, CONV_K, DN_QKV_W), CONV_K ** -0.5),
        "dn_a_log": jnp.log(jax.random.uniform(ks[10], (DEPTH, 2, DN_HEADS), f32, minval=1.0, maxval=16.0)),
        "dn_dt_bias": dt + jnp.log(-jnp.expm1(-dt)),
        "dn_norm_w": 1.0 + nrm(ks[11], (DEPTH, DN_DV), 0.02),
        "na_q_norm_w": 1.0 + nrm(ks[12], (DEPTH, NA_DH), 0.02),
        "na_k_norm_w": 1.0 + nrm(ks[13], (DEPTH, NA_DH), 0.02),
        "na_rpb": nrm(ks[14], (DEPTH, NA_HEADS, 2 * NA_KH_MAX - 1, 2 * NA_KW - 1), 0.1),
        "w_branch_a": nrm(ks[15], (DEPTH, DN_W, D_MODEL), DN_W ** -0.5),
        "w_branch_b": nrm(ks[16], (DEPTH, NA_W, D_MODEL), NA_W ** -0.5),
        "w_out": nrm(ks[17], (DEPTH, D_MODEL, D_MODEL), D_MODEL ** -0.5),
        "norm2_w": 1.0 + nrm(ks[18], (DEPTH, D_MODEL), 0.02),
        "w_router": nrm(ks[19], (DEPTH, D_MODEL, N_EXPERTS), D_MODEL ** -0.5),
        "expert_w1": nrm(ks[20], (DEPTH, N_EXPERTS, D_MODEL, EXPERT_FF), D_MODEL ** -0.5),
        "expert_w3": nrm(ks[21], (DEPTH, N_EXPERTS, D_MODEL, EXPERT_FF), D_MODEL ** -0.5),
        "expert_w2": nrm(ks[22], (DEPTH, N_EXPERTS, EXPERT_FF, D_MODEL), EXPERT_FF ** -0.5),
    }


def reference(x, c, ctx, c_ctx, ada_w, ada_b, norm1_w, w_in, conv_w, dn_a_log, dn_dt_bias, dn_norm_w,
              na_q_norm_w, na_k_norm_w, na_rpb, w_branch_a, w_branch_b, w_out, norm2_w, w_router,
              expert_w1, expert_w3, expert_w2):
    B, L, D = x.shape
    tabs = axial_rope_tables(L)
    split_at = np.cumsum(IN_SIZES)[:-1].tolist()
    for i in range(DEPTH):
        mod = jax.nn.silu(c) @ ada_w[i] + ada_b[i]
        sh1, sc1, g1, sh2, sc2, g2 = [m[:, None, :] for m in jnp.split(mod, 6, axis=-1)]
        mod_c = jax.nn.silu(c_ctx) @ ada_w[i] + ada_b[i]
        sh1c, sc1c = mod_c[:D], mod_c[D:2 * D]
        h = rmsnorm(x, norm1_w[i]) * (1 + sc1) + sh1
        hc = rmsnorm(ctx, norm1_w[i]) * (1 + sc1c) + sh1c
        (dq, dk, dv, dz, dbeta, da, nq, nk, nv, gate_a, gate_b) = jnp.split(h @ w_in[i], split_at, axis=-1)
        (cdq, cdk, cdv, _, cbeta, cda, _, cnk, cnv, _, _) = jnp.split(hc @ w_in[i], split_at, axis=-1)
        y_a = delta_branch((dq, dk, dv, dbeta, da), (cdq, cdk, cdv, cbeta, cda), dz, conv_w[i],
                           dn_a_log[i], dn_dt_bias[i], dn_norm_w[i]) @ w_branch_a[i]
        y_b = na_branch(nq, nk, nv, cnk, cnv, na_q_norm_w[i], na_k_norm_w[i], na_rpb[i], tabs) @ w_branch_b[i]
        y = jax.nn.sigmoid(gate_a) * y_a + jax.nn.sigmoid(gate_b) * y_b
        x = x + g1 * (y @ w_out[i])
        h2 = rmsnorm(x, norm2_w[i]) * (1 + sc2) + sh2
        x = x + g2 * expert_choice_moe(h2, w_router[i], expert_w1[i], expert_w3[i], expert_w2[i])
    return x
```

```python
import functools
import math

import numpy as np
import jax
import jax.numpy as jnp
from jax import lax
from jax.experimental import pallas as pl
from jax.experimental.pallas import tpu as pltpu

F32 = jnp.float32
BF16 = jnp.bfloat16

EPS = 1e-6
NEG_INF = -1e30
LANES = 128
GRID_W = 64
N_HEADS = 16
HEAD_DIM = 128
CONV_K = 5
CONV_HALO = 16
DN_CHUNK = 64
DN_SOLVE_BASE = 16
NA_KH = 8
NA_KW = 16
ROPE_THETA = 10000.0
N_EXPERTS = 16
EC_CAPACITY_FACTOR = 2
VMEM_LIMIT = 56 * 1024 * 1024


def _cparams(*sem):
    return pltpu.CompilerParams(dimension_semantics=sem, vmem_limit_bytes=VMEM_LIMIT)


def _sigmoid(x):
    return 1.0 / (1.0 + jnp.exp(-x))


def _silu(x):
    return x * _sigmoid(x)


def _mod_kernel(c_ref, w_ref, b_ref, o_ref):
    s = _silu(c_ref[...])
    o_ref[...] = jnp.dot(s, w_ref[...], preferred_element_type=F32,
                         precision=lax.Precision.HIGHEST) + b_ref[...]


def _adaln_mod(cvec, ada_w, ada_b, tn=1024):
    m, d = cvec.shape
    n = ada_w.shape[1]
    return pl.pallas_call(
        _mod_kernel,
        grid=(n // tn,),
        in_specs=[pl.BlockSpec((m, d), lambda j: (0, 0)),
                  pl.BlockSpec((d, tn), lambda j: (0, j)),
                  pl.BlockSpec((1, tn), lambda j: (0, j))],
        out_specs=pl.BlockSpec((m, tn), lambda j: (0, j)),
        out_shape=jax.ShapeDtypeStruct((m, n), F32),
        compiler_params=_cparams("parallel"),
        name="adaln_mod",
    )(cvec, ada_w, ada_b.reshape(1, n))


def _norm_proj_kernel(x_ref, nw_ref, sc_ref, sh_ref, w_ref, o_ref, h_ref):
    @pl.when(pl.program_id(1) == 0)
    def _():
        x = x_ref[...]
        y = x * lax.rsqrt(jnp.mean(x * x, axis=-1, keepdims=True) + EPS) * nw_ref[...]
        h_ref[...] = (y * (1.0 + sc_ref[0]) + sh_ref[0]).astype(BF16)

    acc = jnp.dot(h_ref[...], w_ref[...], preferred_element_type=F32)
    for k in range(o_ref.shape[1]):
        o_ref[0, k] = acc[:, k * LANES:(k + 1) * LANES].astype(o_ref.dtype)


def _norm_proj(x, norm_w, scale, shift, w, out_dtype, tm, tn):
    b, l, d = x.shape
    n = w.shape[1]
    tpb = l // tm
    per_sample = scale.shape[0] != 1
    mod_idx = (lambda i, j: (i // tpb, 0, 0)) if per_sample else (lambda i, j: (0, 0, 0))
    return pl.pallas_call(
        _norm_proj_kernel,
        grid=(b * tpb, n // tn),
        in_specs=[pl.BlockSpec((tm, d), lambda i, j: (i, 0)),
                  pl.BlockSpec((1, d), lambda i, j: (0, 0)),
                  pl.BlockSpec((1, 1, d), mod_idx),
                  pl.BlockSpec((1, 1, d), mod_idx),
                  pl.BlockSpec((d, tn), lambda i, j: (0, j))],
        out_specs=pl.BlockSpec((1, tn // LANES, tm, LANES), lambda i, j: (i // tpb, j, i % tpb, 0)),
        out_shape=jax.ShapeDtypeStruct((b, n // LANES, l, LANES), out_dtype),
        scratch_shapes=[pltpu.VMEM((tm, d), BF16)],
        compiler_params=_cparams("parallel", "arbitrary"),
        name="norm_proj",
    )(x.reshape(b * l, d), norm_w.reshape(1, d), scale, shift, w)


def _dn_gates_kernel(x_ref, alog_ref, dtb_ref, o_ref):
    l = x_ref.shape[1]
    lane = lax.broadcasted_iota(jnp.int32, (LANES, LANES), 1)
    row = lax.broadcasted_iota(jnp.int32, (LANES, LANES), 0)
    same_chunk = (row // DN_CHUNK) == (lane // DN_CHUNK)
    prefix_m = jnp.where(same_chunk & (lane <= row), 1.0, 0.0).astype(F32)
    suffix_m = jnp.where(same_chunk & (lane >= row), 1.0, 0.0).astype(F32)
    neg_a = -jnp.exp(alog_ref[...])
    dtb = dtb_ref[...]

    def tile(t, carry):
        rows = pl.ds(pl.multiple_of(t * LANES, LANES), LANES)
        x = x_ref[0, rows, :]
        beta = _sigmoid(x)
        z = x + dtb
        g = neg_a * (jnp.maximum(z, 0.0) + jnp.log1p(jnp.exp(-jnp.abs(z))))
        pre = jnp.dot(prefix_m, g, preferred_element_type=F32, precision=lax.Precision.HIGHEST)
        suf = jnp.dot(suffix_m, g, preferred_element_type=F32, precision=lax.Precision.HIGHEST)
        o_ref[0, rows, :] = jnp.where(lane < 2 * N_HEADS, beta, jnp.where(lane < 3 * N_HEADS, pre, suf))
        return carry

    lax.fori_loop(0, l // LANES, tile, 0)


def _dn_gates(ba, alog_lanes, dtb_lanes):
    b, l, _ = ba.shape
    return pl.pallas_call(
        _dn_gates_kernel,
        grid=(b,),
        in_specs=[pl.BlockSpec((1, l, LANES), lambda i: (i, 0, 0)),
                  pl.BlockSpec((1, LANES), lambda i: (0, 0)),
                  pl.BlockSpec((1, LANES), lambda i: (0, 0))],
        out_specs=pl.BlockSpec((1, l, LANES), lambda i: (i, 0, 0)),
        out_shape=jax.ShapeDtypeStruct((b, l, LANES), F32),
        compiler_params=_cparams("parallel"),
        name="dn_gates",
    )(ba, alog_lanes, dtb_lanes)


def _dn_masks():
    row = lax.broadcasted_iota(jnp.int32, (LANES, LANES), 0)
    col = lax.broadcasted_iota(jnp.int32, (LANES, LANES), 1)
    fwd = row < DN_CHUNK
    same = (row // DN_CHUNK) == (col // DN_CHUNK)
    tril = same & ((fwd & (row >= col)) | (~fwd & (row <= col)))
    strict = tril & (row != col)
    return row, col, tril, strict


def _lane_bcast(x, idx):
    y = pltpu.roll(x, (LANES - idx) % LANES, axis=1)
    return jnp.broadcast_to(y[:, 0:1], x.shape)


def _dn_prepare(h, n_rows, raw_refs, conv_refs, bg_ref, pad_ref, qkv_s, sel_s):
    n_tiles = n_rows // LANES
    zeros_halo = jnp.zeros((CONV_HALO, LANES), F32)
    for which in range(3):
        pad_ref[0:CONV_HALO, :] = zeros_halo
        pad_ref[CONV_HALO + n_rows:2 * CONV_HALO + n_rows, :] = zeros_halo
        pad_ref[CONV_HALO:CONV_HALO + n_rows, :] = raw_refs[which][0, 0].astype(F32)
        taps = conv_refs[which][0]

        def tile(t, carry, which=which, taps=taps):
            base = t * LANES + CONV_HALO - CONV_K // 2
            acc = jnp.zeros((LANES, LANES), F32)
            for j in range(CONV_K):
                acc = acc + pad_ref[pl.ds(base + j, LANES), :] * taps[j:j + 1, :]
            y = _silu(acc)
            if which < 2:
                y = y * lax.rsqrt(jnp.sum(y * y, axis=-1, keepdims=True) + EPS)
            if which == 0:
                y = y * (HEAD_DIM ** -0.5)
            qkv_s[which, pl.ds(pl.multiple_of(t * LANES, LANES), LANES), :] = y
            return carry

        lax.fori_loop(0, n_tiles, tile, 0)

    def sel_tile(t, carry):
        rows = pl.ds(pl.multiple_of(t * LANES, LANES), LANES)
        x = bg_ref[0, rows, :]
        for s in range(4):
            sel_s[s, rows, :] = _lane_bcast(x, s * N_HEADS + h)
        return carry

    lax.fori_loop(0, n_tiles, sel_tile, 0)


def _unit_tri_solve(a, rhs, row, col):
    blk = lambda m: (row // m) == (col // m)
    mm = lambda x, y: jnp.dot(x.astype(BF16), y.astype(BF16), preferred_element_type=F32)
    ad = jnp.where(blk(DN_SOLVE_BASE), a, 0.0)
    t = jnp.where(row == col, 1.0, 0.0) - ad
    p = mm(ad, ad)
    for _ in range(int(math.log2(DN_SOLVE_BASE)) - 2):
        tp = mm(jnp.concatenate([t, p], axis=0), p)
        t = t + tp[:LANES]
        p = tp[LANES:]
    t = t + mm(t, p)
    k = 2 * DN_SOLVE_BASE
    while k < DN_CHUNK:
        t = t - mm(mm(t, jnp.where(blk(k) & ~blk(k // 2), a, 0.0)), t)
        k *= 2
    ny = mm(t, jnp.concatenate([jnp.where(blk(k // 2), 0.0, a), rhs], axis=1))
    y = ny[:, LANES:]
    return y - mm(ny[:, :LANES], y)


def _dn_intra(i, n, qkv_s, sel_s, wq_s, aq_s, kdt_s, u_s, eg_s, with_out):
    row, col, tril, strict = _dn_masks()
    ri = pl.ds(pl.multiple_of(i * DN_CHUNK, DN_CHUNK), DN_CHUNK)
    rj = pl.ds(pl.multiple_of((n - 1 - i) * DN_CHUNK, DN_CHUNK), DN_CHUNK)
    pair = lambda ref, a, b: jnp.concatenate([ref[a, ri, :], ref[b, rj, :]], axis=0)
    q2, k2, v2 = pair(qkv_s, 0, 0), pair(qkv_s, 1, 1), pair(qkv_s, 2, 2)
    b2, g2 = pair(sel_s, 0, 1), pair(sel_s, 2, 3)
    gr = g2.T
    decay = jnp.exp(jnp.where(tril, g2 - gr, -jnp.inf))
    kb2 = k2 * b2
    k2b = k2.astype(BF16)
    nt = (((1,), (1,)), ((), ()))
    if with_out:
        kk = lax.dot_general(jnp.concatenate([kb2, q2], axis=0).astype(BF16), k2b, nt,
                             preferred_element_type=F32)
        a_kk, a_qk = kk[:LANES], jnp.where(tril, kk[LANES:] * decay, 0.0)
    else:
        a_kk = lax.dot_general(kb2.astype(BF16), k2b, nt, preferred_element_type=F32)
    a = jnp.where(strict, a_kk * decay, 0.0)
    eg = jnp.exp(g2)
    sol = _unit_tri_solve(a, jnp.concatenate([v2 * b2, kb2 * eg], axis=1), row, col)
    u2, w2 = sol[:, :LANES], sol[:, LANES:]
    gl_f, gl_b = g2[DN_CHUNK - 1:DN_CHUNK, :], g2[DN_CHUNK:DN_CHUNK + 1, :]
    gl = jnp.concatenate([jnp.broadcast_to(gl_f, (DN_CHUNK, LANES)),
                          jnp.broadcast_to(gl_b, (DN_CHUNK, LANES))], axis=0)
    kd2 = k2 * jnp.exp(gl - g2)
    kdt_s[i] = kd2.T.astype(BF16)
    u_s[i] = u2
    eg_s[i] = jnp.exp(jnp.concatenate([gl_f, gl_b, jnp.zeros((6, LANES), F32)], axis=0))
    if with_out:
        qd2 = q2 * eg
        wq_s[i, 0] = jnp.concatenate([w2[:DN_CHUNK], qd2[:DN_CHUNK]], axis=0).astype(BF16)
        wq_s[i, 1] = jnp.concatenate([w2[DN_CHUNK:], qd2[DN_CHUNK:]], axis=0).astype(BF16)
        aq_s[i] = a_qk.astype(BF16)
    else:
        wq_s[i, 0] = jnp.concatenate([w2[:DN_CHUNK], w2[:DN_CHUNK]], axis=0).astype(BF16)
        wq_s[i, 1] = jnp.concatenate([w2[DN_CHUNK:], w2[DN_CHUNK:]], axis=0).astype(BF16)


def _dn_recur(i, n, s_f, s_b, wq_s, aq_s, kdt_s, u_s, eg_s, o_s, with_out):
    row = lax.broadcasted_iota(jnp.int32, (LANES, LANES), 0)
    r_f = jnp.dot(wq_s[i, 0], s_f.astype(BF16), preferred_element_type=F32)
    r_b = jnp.dot(wq_s[i, 1], s_b.astype(BF16), preferred_element_type=F32)
    v_new = u_s[i] - jnp.concatenate([r_f[:DN_CHUNK], r_b[:DN_CHUNK]], axis=0)
    if with_out:
        o2 = (jnp.concatenate([r_f[DN_CHUNK:], r_b[DN_CHUNK:]], axis=0)
              + jnp.dot(aq_s[i], v_new.astype(BF16), preferred_element_type=F32))
        o_s[0, pl.ds(pl.multiple_of(i * DN_CHUNK, DN_CHUNK), DN_CHUNK), :] = o2[:DN_CHUNK]
        o_s[1, pl.ds(pl.multiple_of((n - 1 - i) * DN_CHUNK, DN_CHUNK), DN_CHUNK), :] = o2[DN_CHUNK:]
    v_cat = jnp.concatenate([jnp.where(row < DN_CHUNK, v_new, 0.0),
                             jnp.where(row >= DN_CHUNK, v_new, 0.0)], axis=1).astype(BF16)
    ds = jnp.dot(kdt_s[i], v_cat, preferred_element_type=F32)
    eg = eg_s[i]
    s_f = s_f * jnp.broadcast_to(eg[0:1, :], (LANES, LANES)) + ds[:, :LANES]
    s_b = s_b * jnp.broadcast_to(eg[1:2, :], (LANES, LANES)) + ds[:, LANES:]
    return s_f, s_b


def _delta_kernel(q_ref, k_ref, v_ref, z_ref, cq_ref, ck_ref, cv_ref, bg_ref, cbg_ref,
                  wq_ref, wk_ref, wv_ref, nw_ref, o_ref,
                  pad_s, qkv_s, sel_s, cqkv_s, csel_s, wq_s, aq_s, kdt_s, u_s, eg_s, o_s):
    h = pl.program_id(1)
    l = q_ref.shape[2]
    lc = cq_ref.shape[2]
    n, nc = l // DN_CHUNK, lc // DN_CHUNK
    conv_refs = (wq_ref, wk_ref, wv_ref)
    stage = (wq_s, aq_s, kdt_s, u_s, eg_s)

    _dn_prepare(h, lc, (cq_ref, ck_ref, cv_ref), conv_refs, cbg_ref, pad_s, cqkv_s, csel_s)

    def c_intra(i, c):
        _dn_intra(i, nc, cqkv_s, csel_s, *stage, with_out=False)
        return c

    lax.fori_loop(0, nc, c_intra, 0)
    zero = jnp.zeros((LANES, LANES), F32)
    s_f, s_b = lax.fori_loop(
        0, nc, lambda i, s: _dn_recur(i, nc, s[0], s[1], *stage, o_s, with_out=False), (zero, zero))

    _dn_prepare(h, l, (q_ref, k_ref, v_ref), conv_refs, bg_ref, pad_s, qkv_s, sel_s)

    def l_intra(i, c):
        _dn_intra(i, n, qkv_s, sel_s, *stage, with_out=True)
        return c

    lax.fori_loop(0, n, l_intra, 0, unroll=2)
    lax.fori_loop(
        0, n, lambda i, s: _dn_recur(i, n, s[0], s[1], *stage, o_s, with_out=True), (s_f, s_b))

    nw = nw_ref[...]

    def out_tile(t, c):
        rows = pl.ds(pl.multiple_of(t * LANES, LANES), LANES)
        o = o_s[0, rows, :] + o_s[1, rows, :]
        y = o * lax.rsqrt(jnp.mean(o * o, axis=-1, keepdims=True) + EPS) * nw
        o_ref[0, 0, rows, :] = (y * _silu(z_ref[0, 0, rows, :].astype(F32))).astype(o_ref.dtype)
        return c

    lax.fori_loop(0, l // LANES, out_tile, 0)


def _delta_branch(p_lat, p_ctx, bg, cbg, conv_taps, norm_w):
    b, _, l, _ = p_lat.shape
    lc = p_ctx.shape[2]
    n = l // DN_CHUNK
    hb = lambda off: pl.BlockSpec((1, 1, l, LANES), lambda i, j, off=off: (i, off + j, 0, 0))
    cb = lambda off: pl.BlockSpec((1, 1, lc, LANES), lambda i, j, off=off: (i, off + j, 0, 0))
    tb = lambda off: pl.BlockSpec((1, 8, LANES), lambda i, j, off=off: (off + j, 0, 0))
    return pl.pallas_call(
        _delta_kernel,
        grid=(b, N_HEADS),
        in_specs=[hb(0), hb(N_HEADS), hb(2 * N_HEADS), hb(3 * N_HEADS),
                  cb(0), cb(N_HEADS), cb(2 * N_HEADS),
                  pl.BlockSpec((1, l, LANES), lambda i, j: (i, 0, 0)),
                  pl.BlockSpec((1, lc, LANES), lambda i, j: (i, 0, 0)),
                  tb(0), tb(N_HEADS), tb(2 * N_HEADS),
                  pl.BlockSpec((1, LANES), lambda i, j: (0, 0))],
        out_specs=pl.BlockSpec((1, 1, l, LANES), lambda i, j: (i, j, 0, 0)),
        out_shape=jax.ShapeDtypeStruct((b, N_HEADS, l, LANES), BF16),
        scratch_shapes=[pltpu.VMEM((l + 2 * CONV_HALO, LANES), F32),
                        pltpu.VMEM((3, l, LANES), F32),
                        pltpu.VMEM((4, l, LANES), F32),
                        pltpu.VMEM((3, lc, LANES), F32),
                        pltpu.VMEM((4, lc, LANES), F32),
                        pltpu.VMEM((n, 2, LANES, LANES), BF16),
                        pltpu.VMEM((n, LANES, LANES), BF16),
                        pltpu.VMEM((n, LANES, LANES), BF16),
                        pltpu.VMEM((n, LANES, LANES), F32),
                        pltpu.VMEM((n, 8, LANES), F32),
                        pltpu.VMEM((2, l, LANES), F32)],
        compiler_params=_cparams("parallel", "arbitrary"),
        name="delta_branch",
    )(p_lat, p_lat, p_lat, p_lat, p_ctx, p_ctx, p_ctx, bg, cbg,
      conv_taps, conv_taps, conv_taps, norm_w.reshape(1, LANES))


def _rope(x, cos, sin):
    lane = lax.broadcasted_iota(jnp.int32, x.shape, 1)
    quarter = HEAD_DIM // 4
    partner = jnp.where((lane // quarter) % 2 == 0,
                        pltpu.roll(x, LANES - quarter, axis=1), pltpu.roll(x, quarter, axis=1))
    return x * cos + partner * sin


def _head_rms(x, w):
    return x * lax.rsqrt(jnp.mean(x * x, axis=-1, keepdims=True) + EPS) * w


def _na_kernel(q_ref, k_ref, v_ref, ck_ref, cv_ref, qnw_ref, knw_ref, cos_ref, sin_ref, bias_ref, o_ref,
               q_s, k_s, ck_s):
    l = q_ref.shape[2]
    rows = l // GRID_W
    kh = min(NA_KH, rows)
    n_loc = kh * GRID_W
    qnw, knw = qnw_ref[...], knw_ref[...]

    def prep(t, c):
        r = pl.ds(pl.multiple_of(t * LANES, LANES), LANES)
        cos, sin = cos_ref[r, :], sin_ref[r, :]
        q = _rope(_head_rms(q_ref[0, 0, r, :].astype(F32), qnw), cos, sin) * (HEAD_DIM ** -0.5)
        q_s[r, :] = q.astype(BF16)
        k_s[r, :] = _rope(_head_rms(k_ref[0, 0, r, :].astype(F32), knw), cos, sin).astype(BF16)
        return c

    lax.fori_loop(0, l // LANES, prep, 0)
    ck_s[...] = _head_rms(ck_ref[0, 0].astype(F32), knw).astype(BF16)
    nt = (((1,), (1,)), ((), ()))

    def row_block(r, c):
        rs = jnp.clip(r - kh // 2, 0, rows - kh)
        qr = q_s[pl.ds(pl.multiple_of(r * GRID_W, GRID_W), GRID_W), :]
        kloc = pl.ds(pl.multiple_of(rs * GRID_W, GRID_W), n_loc)
        s_loc = lax.dot_general(qr, k_s[kloc, :], nt, preferred_element_type=F32) + bias_ref[0, r - rs]
        s_ctx = lax.dot_general(qr, ck_s[...], nt, preferred_element_type=F32)
        m = jnp.maximum(jnp.max(s_loc, axis=-1, keepdims=True), jnp.max(s_ctx, axis=-1, keepdims=True))
        p_loc, p_ctx = jnp.exp(s_loc - m), jnp.exp(s_ctx - m)
        denom = jnp.sum(p_loc, axis=-1, keepdims=True) + jnp.sum(p_ctx, axis=-1, keepdims=True)
        o = (jnp.dot(p_loc.astype(BF16), v_ref[0, 0, kloc, :], preferred_element_type=F32)
             + jnp.dot(p_ctx.astype(BF16), cv_ref[0, 0], preferred_element_type=F32))
        o_ref[0, 0, pl.ds(pl.multiple_of(r * GRID_W, GRID_W), GRID_W), :] = (o / denom).astype(o_ref.dtype)
        return c

    lax.fori_loop(0, rows, row_block, 0)


def _na_tables(l):
    pos = jnp.arange(l)
    row = (pos // GRID_W).astype(F32)
    col = (pos % GRID_W).astype(F32)
    half = HEAD_DIM // 2
    inv_freq = ROPE_THETA ** (-jnp.arange(0, half, 2, dtype=F32) / half)
    ang_r = row[:, None] * inv_freq[None, :]
    ang_c = col[:, None] * inv_freq[None, :]
    cos = jnp.concatenate([jnp.cos(ang_r), jnp.cos(ang_r), jnp.cos(ang_c), jnp.cos(ang_c)], axis=-1)
    sin = jnp.concatenate([-jnp.sin(ang_r), jnp.sin(ang_r), -jnp.sin(ang_c), jnp.sin(ang_c)], axis=-1)
    return cos, sin


def _na_bias_table(rpb, rows):
    kh = min(NA_KH, rows)
    t = np.arange(NA_KH)[:, None]
    i = np.arange(kh)[None, :]
    dr = np.clip(i - t + NA_KH - 1, 0, 2 * NA_KH - 2)
    q = np.arange(GRID_W)[:, None]
    kc = np.arange(GRID_W)[None, :]
    qstart = np.clip(q - NA_KW // 2, 0, GRID_W - NA_KW)
    in_win = (kc >= qstart) & (kc < qstart + NA_KW)
    dc = np.clip(kc - q + NA_KW - 1, 0, 2 * NA_KW - 2)
    tab = rpb.astype(F32)[:, dr[:, :, None, None], dc[None, None, :, :]]
    tab = jnp.where(jnp.asarray(in_win)[None, None, None], tab, NEG_INF)
    return tab.transpose(0, 1, 3, 2, 4).reshape(rpb.shape[0], NA_KH, GRID_W, kh * GRID_W)


def _na_branch(p_lat, p_ctx, q_norm_w, k_norm_w, rpb):
    b, _, l, _ = p_lat.shape
    lc = p_ctx.shape[2]
    rows = l // GRID_W
    n_loc = min(NA_KH, rows) * GRID_W
    cos, sin = _na_tables(l)
    bias = _na_bias_table(rpb, rows)
    hb = lambda off: pl.BlockSpec((1, 1, l, LANES), lambda j, i, off=off: (i, off + j, 0, 0))
    cb = lambda off: pl.BlockSpec((1, 1, lc, LANES), lambda j, i, off=off: (i, off + j, 0, 0))
    const = lambda shape: pl.BlockSpec(shape, lambda j, i: (0,) * len(shape))
    return pl.pallas_call(
        _na_kernel,
        grid=(N_HEADS, b),
        in_specs=[hb(4 * N_HEADS), hb(5 * N_HEADS), hb(6 * N_HEADS), cb(3 * N_HEADS), cb(4 * N_HEADS),
                  const((1, LANES)), const((1, LANES)), const((l, LANES)), const((l, LANES)),
                  pl.BlockSpec((1, NA_KH, GRID_W, n_loc), lambda j, i: (j, 0, 0, 0))],
        out_specs=pl.BlockSpec((1, 1, l, LANES), lambda j, i: (i, j, 0, 0)),
        out_shape=jax.ShapeDtypeStruct((b, N_HEADS, l, LANES), BF16),
        scratch_shapes=[pltpu.VMEM((l, LANES), BF16), pltpu.VMEM((l, LANES), BF16),
                        pltpu.VMEM((lc, LANES), BF16)],
        compiler_params=_cparams("parallel", "arbitrary"),
        name="na_branch",
    )(p_lat, p_lat, p_lat, p_ctx, p_ctx, q_norm_w.reshape(1, LANES), k_norm_w.reshape(1, LANES),
      cos, sin, bias)


def _merge_kernel(a_ref, b_ref, wa_ref, wb_ref, ga_ref, gb_ref, o_ref, a_s, b_s):
    @pl.when(pl.program_id(1) == 0)
    def _():
        for k in range(N_HEADS):
            a_s[:, k * LANES:(k + 1) * LANES] = a_ref[0, k]
            b_s[:, k * LANES:(k + 1) * LANES] = b_ref[0, k]

    ya = jnp.dot(a_s[...], wa_ref[...], preferred_element_type=F32)
    yb = jnp.dot(b_s[...], wb_ref[...], preferred_element_type=F32)
    for k in range(ga_ref.shape[1]):
        cols = slice(k * LANES, (k + 1) * LANES)
        o_ref[:, cols] = (_sigmoid(ga_ref[0, k].astype(F32)) * ya[:, cols]
                          + _sigmoid(gb_ref[0, k].astype(F32)) * yb[:, cols]).astype(o_ref.dtype)


def _merge(dn_o, na_o, w_a, w_b, p_lat, tm=512, tn=512):
    b, _, l, _ = dn_o.shape
    d = w_a.shape[1]
    tpb = l // tm
    nb = tn // LANES
    head_blk = pl.BlockSpec((1, N_HEADS, tm, LANES), lambda i, j: (i // tpb, 0, i % tpb, 0))
    gate_blk = lambda off: pl.BlockSpec((1, nb, tm, LANES),
                                        lambda i, j, off=off: (i // tpb, off // nb + j, i % tpb, 0))
    w_blk = pl.BlockSpec((N_HEADS * LANES, tn), lambda i, j: (0, j))
    return pl.pallas_call(
        _merge_kernel,
        grid=(b * tpb, d // tn),
        in_specs=[head_blk, head_blk, w_blk, w_blk, gate_blk(7 * N_HEADS), gate_blk(8 * N_HEADS)],
        out_specs=pl.BlockSpec((tm, tn), lambda i, j: (i, j)),
        out_shape=jax.ShapeDtypeStruct((b * l, d), BF16),
        scratch_shapes=[pltpu.VMEM((tm, N_HEADS * LANES), BF16), pltpu.VMEM((tm, N_HEADS * LANES), BF16)],
        compiler_params=_cparams("parallel", "arbitrary"),
        name="merge",
    )(dn_o, na_o, w_a, w_b, p_lat, p_lat)


def _out_kernel(y_ref, w_ref, x_ref, g1_ref, nw_ref, sc_ref, sh_ref, wr_ref, x1_ref, h2_ref, lg_ref):
    x1 = x_ref[...] + g1_ref[0] * jnp.dot(y_ref[...], w_ref[...], preferred_element_type=F32)
    x1_ref[...] = x1
    h2 = (x1 * lax.rsqrt(jnp.mean(x1 * x1, axis=-1, keepdims=True) + EPS) * nw_ref[...]
          * (1.0 + sc_ref[0]) + sh_ref[0])
    h2_ref[...] = h2.astype(BF16)
    lg_ref[...] = jnp.dot(h2, wr_ref[...], preferred_element_type=F32, precision=lax.Precision.HIGHEST)


def _out_proj(y, w_out, x2d, g1, norm_w, scale, shift, w_router_pad, l, tm=256):
    m, d = x2d.shape
    tpb = l // tm
    row_blk = lambda: pl.BlockSpec((tm, d), lambda i: (i, 0))
    mod_blk = lambda: pl.BlockSpec((1, 1, d), lambda i: (i // tpb, 0, 0))
    return pl.pallas_call(
        _out_kernel,
        grid=(m // tm,),
        in_specs=[row_blk(), pl.BlockSpec((d, d), lambda i: (0, 0)), row_blk(), mod_blk(),
                  pl.BlockSpec((1, d), lambda i: (0, 0)), mod_blk(), mod_blk(),
                  pl.BlockSpec((d, LANES), lambda i: (0, 0))],
        out_specs=[row_blk(), row_blk(), pl.BlockSpec((tm, LANES), lambda i: (i, 0))],
        out_shape=[jax.ShapeDtypeStruct((m, d), F32), jax.ShapeDtypeStruct((m, d), BF16),
                   jax.ShapeDtypeStruct((m, LANES), F32)],
        compiler_params=_cparams("parallel"),
        name="out_proj",
    )(y, w_out, x2d, g1, norm_w.reshape(1, d), scale, shift, w_router_pad)


def _ffn_kernel(x_ref, w1_ref, w3_ref, w2_ref, g_ref, o_ref):
    x = x_ref[0, 0]
    h1 = jnp.dot(x, w1_ref[0], preferred_element_type=F32)
    h3 = jnp.dot(x, w3_ref[0], preferred_element_type=F32)
    hid = (_silu(h1) * h3).astype(BF16)
    o_ref[0, 0] = jnp.dot(hid, w2_ref[0], preferred_element_type=F32) * g_ref[0, 0]


def _expert_ffn(xg, w1, w3, w2, gate):
    b, e, c, d = xg.shape
    f = w1.shape[2]
    tok = lambda last: pl.BlockSpec((1, 1, c, last), lambda j, i: (i, j, 0, 0))
    return pl.pallas_call(
        _ffn_kernel,
        grid=(e, b),
        in_specs=[tok(d), pl.BlockSpec((1, d, f), lambda j, i: (j, 0, 0)),
                  pl.BlockSpec((1, d, f), lambda j, i: (j, 0, 0)),
                  pl.BlockSpec((1, f, d), lambda j, i: (j, 0, 0)), tok(1)],
        out_specs=tok(d),
        out_shape=jax.ShapeDtypeStruct((b, e, c, d), F32),
        compiler_params=_cparams("parallel", "arbitrary"),
        name="expert_ffn",
    )(xg, w1, w3, w2, gate)


def _layer(x, ctx, mod, mod_c, norm1_w, w_in, conv_w, a_log, dt_bias, dn_norm_w, q_norm_w, k_norm_w, rpb,
           w_a, w_b, w_out, norm2_w, w_router, w1, w3, w2):
    b, l, d = x.shape
    sh1, sc1, g1, sh2, sc2, g2 = [m[:, None, :] for m in jnp.split(mod, 6, axis=-1)]
    sh1c, sc1c = mod_c[None, None, :d], mod_c[None, None, d:2 * d]

    hd = N_HEADS * HEAD_DIM
    offs = np.cumsum([0, hd, hd, hd, hd, 2 * N_HEADS, 2 * N_HEADS, hd, hd, hd, d, d])
    col = lambda k: w_in[:, offs[k]:offs[k + 1]]
    w_main = jnp.concatenate([col(k) for k in (0, 1, 2, 3, 6, 7, 8, 9, 10)], axis=1).astype(BF16)
    w_ctx = jnp.concatenate([col(k) for k in (0, 1, 2, 7, 8)], axis=1).astype(BF16)
    w_ba = jnp.pad(jnp.concatenate([col(4), col(5)], axis=1), ((0, 0), (0, LANES - 4 * N_HEADS))).astype(BF16)

    p_lat = _norm_proj(x, norm1_w, sc1, sh1, w_main, BF16, tm=1024, tn=512)
    p_ctx = _norm_proj(ctx, norm1_w, sc1c, sh1c, w_ctx, BF16, tm=ctx.shape[1], tn=512)
    ba = _norm_proj(x, norm1_w, sc1, sh1, w_ba, F32, tm=1024, tn=LANES).reshape(b, l, LANES)
    cba = _norm_proj(ctx, norm1_w, sc1c, sh1c, w_ba, F32, tm=ctx.shape[1], tn=LANES).reshape(
        b, ctx.shape[1], LANES)

    lanes_pad = lambda v: jnp.pad(v.reshape(1, -1).astype(F32), ((0, 0), (2 * N_HEADS, LANES - 4 * N_HEADS)))
    alog_l, dtb_l = lanes_pad(a_log), lanes_pad(dt_bias)
    bg, cbg = _dn_gates(ba, alog_l, dtb_l), _dn_gates(cba, alog_l, dtb_l)
    taps = jnp.pad(conv_w.astype(F32), ((0, 8 - CONV_K), (0, 0))).reshape(8, 3 * N_HEADS, LANES).transpose(1, 0, 2)
    dn_o = _delta_branch(p_lat, p_ctx, bg, cbg, taps, dn_norm_w.astype(F32))
    na_o = _na_branch(p_lat, p_ctx, q_norm_w.astype(F32), k_norm_w.astype(F32), rpb)

    y = _merge(dn_o, na_o, w_a.astype(BF16), w_b.astype(BF16), p_lat)
    w_router_pad = jnp.pad(w_router.astype(F32), ((0, 0), (0, LANES - N_EXPERTS)))
    x1, h2, logits = _out_proj(y, w_out.astype(BF16), x.reshape(b * l, d), g1, norm2_w, sc2, sh2,
                               w_router_pad, l)

    cap = EC_CAPACITY_FACTOR * l // N_EXPERTS
    aff = jax.nn.softmax(logits[:, :N_EXPERTS].reshape(b, l, N_EXPERTS), axis=-1)
    gate, idx = lax.top_k(aff.transpose(0, 2, 1), cap)
    bidx = jnp.arange(b)[:, None, None]
    xg = h2.reshape(b, l, d)[bidx, idx]
    ye = _expert_ffn(xg, w1.astype(BF16), w3.astype(BF16), w2.astype(BF16), gate[..., None])
    moe = jnp.zeros((b, l, d), F32).at[bidx, idx].add(ye)
    return x1.reshape(b, l, d) + g2 * moe


def kernel(x, c, ctx, c_ctx, ada_w, ada_b, norm1_w, w_in, conv_w, dn_a_log, dn_dt_bias, dn_norm_w,
           na_q_norm_w, na_k_norm_w, na_rpb, w_branch_a, w_branch_b, w_out, norm2_w, w_router,
           expert_w1, expert_w3, expert_w2):
    b = x.shape[0]
    depth = ada_w.shape[0]
    cvec = jnp.concatenate([c, c_ctx[None, :], jnp.zeros((16 - b - 1, c.shape[1]), c.dtype)], axis=0)
    for i in range(depth):
        mod_all = _adaln_mod(cvec, ada_w[i], ada_b[i])
        x = _layer(x, ctx, mod_all[:b], mod_all[b], norm1_w[i], w_in[i], conv_w[i], dn_a_log[i],
                   dn_dt_bias[i], dn_norm_w[i], na_q_norm_w[i], na_k_norm_w[i], na_rpb[i],
                   w_branch_a[i], w_branch_b[i], w_out[i], norm2_w[i], w_router[i],
                   expert_w1[i], expert_w3[i], expert_w2[i])
    return x
```

```python
import functools
import math

import numpy as np
import jax
import jax.numpy as jnp
from jax import lax
from jax.experimental import pallas as pl
from jax.experimental.pallas import tpu as pltpu

F32 = jnp.float32
BF16 = jnp.bfloat16

EPS = 1e-6
NEG_INF = -1e30
LANES = 128
GRID_W = 64
N_HEADS = 16
HEAD_DIM = 128
CONV_K = 5
CONV_HALO = 16
DN_CHUNK = 64
DN_GROUP = 8
DN_SOLVE_BASE = 16
NA_KH = 8
NA_KW = 16
NA_ROW_GROUP = 8
ROPE_THETA = 10000.0
N_EXPERTS = 16
EC_CAPACITY_FACTOR = 2
VMEM_LIMIT = 56 * 1024 * 1024


def _cparams(*sem):
    return pltpu.CompilerParams(dimension_semantics=sem, vmem_limit_bytes=VMEM_LIMIT)


def _sigmoid(x):
    return 1.0 / (1.0 + jnp.exp(-x))


def _silu(x):
    return x * _sigmoid(x)


def _mod_kernel(c_ref, w_ref, b_ref, o_ref):
    s = _silu(c_ref[...])
    o_ref[...] = jnp.dot(s, w_ref[...], preferred_element_type=F32,
                         precision=lax.Precision.HIGHEST) + b_ref[...]


def _adaln_mod(cvec, ada_w, ada_b, tn=1024):
    m, d = cvec.shape
    n = ada_w.shape[1]
    return pl.pallas_call(
        _mod_kernel,
        grid=(n // tn,),
        in_specs=[pl.BlockSpec((m, d), lambda j: (0, 0)),
                  pl.BlockSpec((d, tn), lambda j: (0, j)),
                  pl.BlockSpec((1, tn), lambda j: (0, j))],
        out_specs=pl.BlockSpec((m, tn), lambda j: (0, j)),
        out_shape=jax.ShapeDtypeStruct((m, n), F32),
        compiler_params=_cparams("parallel"),
        name="adaln_mod",
    )(cvec, ada_w, ada_b.reshape(1, n))


def _norm_proj_kernel(x_ref, nw_ref, sc_ref, sh_ref, w_ref, o_ref, h_ref):
    @pl.when(pl.program_id(1) == 0)
    def _():
        x = x_ref[...]
        y = x * lax.rsqrt(jnp.mean(x * x, axis=-1, keepdims=True) + EPS) * nw_ref[...]
        h_ref[...] = (y * (1.0 + sc_ref[0]) + sh_ref[0]).astype(BF16)

    acc = jnp.dot(h_ref[...], w_ref[...], preferred_element_type=F32)
    for k in range(o_ref.shape[1]):
        o_ref[0, k] = acc[:, k * LANES:(k + 1) * LANES].astype(o_ref.dtype)


def _norm_proj(x, norm_w, scale, shift, w, out_dtype, tm, tn):
    b, l, d = x.shape
    n = w.shape[1]
    tpb = l // tm
    per_sample = scale.shape[0] != 1
    mod_idx = (lambda i, j: (i // tpb, 0, 0)) if per_sample else (lambda i, j: (0, 0, 0))
    return pl.pallas_call(
        _norm_proj_kernel,
        grid=(b * tpb, n // tn),
        in_specs=[pl.BlockSpec((tm, d), lambda i, j: (i, 0)),
                  pl.BlockSpec((1, d), lambda i, j: (0, 0)),
                  pl.BlockSpec((1, 1, d), mod_idx),
                  pl.BlockSpec((1, 1, d), mod_idx),
                  pl.BlockSpec((d, tn), lambda i, j: (0, j))],
        out_specs=pl.BlockSpec((1, tn // LANES, tm, LANES), lambda i, j: (i // tpb, j, i % tpb, 0)),
        out_shape=jax.ShapeDtypeStruct((b, n // LANES, l, LANES), out_dtype),
        scratch_shapes=[pltpu.VMEM((tm, d), BF16)],
        compiler_params=_cparams("parallel", "arbitrary"),
        name="norm_proj",
    )(x.reshape(b * l, d), norm_w.reshape(1, d), scale, shift, w)


def _dn_gates_kernel(x_ref, alog_ref, dtb_ref, o_ref):
    l = x_ref.shape[1]
    lane = lax.broadcasted_iota(jnp.int32, (LANES, LANES), 1)
    row = lax.broadcasted_iota(jnp.int32, (LANES, LANES), 0)
    same_chunk = (row // DN_CHUNK) == (lane // DN_CHUNK)
    prefix_m = jnp.where(same_chunk & (lane <= row), 1.0, 0.0).astype(F32)
    suffix_m = jnp.where(same_chunk & (lane >= row), 1.0, 0.0).astype(F32)
    neg_a = -jnp.exp(alog_ref[...])
    dtb = dtb_ref[...]

    def tile(t, carry):
        rows = pl.ds(pl.multiple_of(t * LANES, LANES), LANES)
        x = x_ref[0, rows, :]
        beta = _sigmoid(x)
        z = x + dtb
        g = neg_a * (jnp.maximum(z, 0.0) + jnp.log1p(jnp.exp(-jnp.abs(z))))
        pre = jnp.dot(prefix_m, g, preferred_element_type=F32, precision=lax.Precision.HIGHEST)
        suf = jnp.dot(suffix_m, g, preferred_element_type=F32, precision=lax.Precision.HIGHEST)
        o_ref[0, rows, :] = jnp.where(lane < 2 * N_HEADS, beta, jnp.where(lane < 3 * N_HEADS, pre, suf))
        return carry

    lax.fori_loop(0, l // LANES, tile, 0)


def _dn_gates(ba, alog_lanes, dtb_lanes):
    b, l, _ = ba.shape
    return pl.pallas_call(
        _dn_gates_kernel,
        grid=(b,),
        in_specs=[pl.BlockSpec((1, l, LANES), lambda i: (i, 0, 0)),
                  pl.BlockSpec((1, LANES), lambda i: (0, 0)),
                  pl.BlockSpec((1, LANES), lambda i: (0, 0))],
        out_specs=pl.BlockSpec((1, l, LANES), lambda i: (i, 0, 0)),
        out_shape=jax.ShapeDtypeStruct((b, l, LANES), F32),
        compiler_params=_cparams("parallel"),
        name="dn_gates",
    )(ba, alog_lanes, dtb_lanes)


def _dn_masks():
    row = lax.broadcasted_iota(jnp.int32, (LANES, LANES), 0)
    col = lax.broadcasted_iota(jnp.int32, (LANES, LANES), 1)
    fwd = row < DN_CHUNK
    same = (row // DN_CHUNK) == (col // DN_CHUNK)
    tril = same & ((fwd & (row >= col)) | (~fwd & (row <= col)))
    strict = tril & (row != col)
    return row, col, tril, strict


def _dn_prepare(h, n_rows, raw_refs, conv_refs, bg_ref, pad_ref, qkv_s, sel_s):
    n_tiles = n_rows // LANES
    zeros_halo = jnp.zeros((CONV_HALO, LANES), F32)
    for which in range(3):
        pad_ref[0:CONV_HALO, :] = zeros_halo
        pad_ref[CONV_HALO + n_rows:2 * CONV_HALO + n_rows, :] = zeros_halo
        pad_ref[CONV_HALO:CONV_HALO + n_rows, :] = raw_refs[which][0, 0].astype(F32)
        taps = conv_refs[which][0]

        def tile(t, carry, which=which, taps=taps):
            base = t * LANES + CONV_HALO - CONV_K // 2
            acc = jnp.zeros((LANES, LANES), F32)
            for j in range(CONV_K):
                acc = acc + pad_ref[pl.ds(base + j, LANES), :] * taps[j:j + 1, :]
            y = _silu(acc)
            if which < 2:
                y = y * lax.rsqrt(jnp.sum(y * y, axis=-1, keepdims=True) + EPS)
            if which == 0:
                y = y * (HEAD_DIM ** -0.5)
            qkv_s[which, pl.ds(pl.multiple_of(t * LANES, LANES), LANES), :] = y
            return carry

        lax.fori_loop(0, n_tiles, tile, 0)

    def sel_tile(t, carry):
        rows = pl.ds(pl.multiple_of(t * LANES, LANES), LANES)
        x = pltpu.roll(bg_ref[0, rows, :], (LANES - h) % LANES, axis=1)
        for s in range(4):
            sel_s[s, rows, :] = jnp.broadcast_to(x[:, s * N_HEADS:s * N_HEADS + 1], x.shape)
        return carry

    lax.fori_loop(0, n_tiles, sel_tile, 0)


def _mm(x, y):
    return jnp.dot(x.astype(BF16), y.astype(BF16), preferred_element_type=F32)


def _unit_tri_solve(a_all, rhs_all, row, col):
    blk = lambda m: (row // m) == (col // m)
    eye = jnp.where(row == col, 1.0, 0.0)
    ad = [jnp.where(blk(DN_SOLVE_BASE), a, 0.0) for a in a_all]
    t = [eye - x for x in ad]
    p = [_mm(x, x) for x in ad]
    for _ in range(int(math.log2(DN_SOLVE_BASE)) - 2):
        tp = [_mm(jnp.concatenate([ti, pi], axis=0), pi) for ti, pi in zip(t, p)]
        t = [ti + x[:LANES] for ti, x in zip(t, tp)]
        p = [x[LANES:] for x in tp]
    tp = [_mm(ti, pi) for ti, pi in zip(t, p)]
    t = [ti + x for ti, x in zip(t, tp)]
    k = 2 * DN_SOLVE_BASE
    while k < DN_CHUNK:
        nk = [_mm(ti, jnp.where(blk(k) & ~blk(k // 2), a, 0.0)) for ti, a in zip(t, a_all)]
        tn = [_mm(x, ti) for x, ti in zip(nk, t)]
        t = [ti - x for ti, x in zip(t, tn)]
        k *= 2
    ny = [_mm(ti, jnp.concatenate([jnp.where(blk(k // 2), 0.0, a), r], axis=1))
          for ti, a, r in zip(t, a_all, rhs_all)]
    ny2 = [_mm(x[:, :LANES], x[:, LANES:]) for x in ny]
    return [x[:, LANES:] - z for x, z in zip(ny, ny2)]


def _dn_intra(steps, n, qkv_s, sel_s, lhs_s, c_s, o0_s, eg_s, with_out):
    row, col, tril, strict = _dn_masks()
    is_f = row < DN_CHUNK
    nt = (((1,), (1,)), ((), ()))

    def load(i):
        ri = pl.ds(pl.multiple_of(i * DN_CHUNK, DN_CHUNK), DN_CHUNK)
        rj = pl.ds(pl.multiple_of((n - 1 - i) * DN_CHUNK, DN_CHUNK), DN_CHUNK)
        pair = lambda ref, a, b: jnp.concatenate([ref[a, ri, :], ref[b, rj, :]], axis=0)
        q2, k2, v2 = pair(qkv_s, 0, 0), pair(qkv_s, 1, 1), pair(qkv_s, 2, 2)
        b2, g2 = pair(sel_s, 0, 1), pair(sel_s, 2, 3)
        decay = jnp.exp(jnp.where(tril, g2 - g2.T, -jnp.inf))
        return dict(q2=q2, k2=k2, v2=v2, b2=b2, g2=g2, decay=decay, kb2=k2 * b2, eg=jnp.exp(g2))

    st = [load(i) for i in steps]
    if with_out:
        kk = [lax.dot_general(jnp.concatenate([d["kb2"], d["q2"]], axis=0).astype(BF16), d["k2"].astype(BF16), nt,
                              preferred_element_type=F32) for d in st]
        a_qk = [jnp.where(tril, x[LANES:] * d["decay"], 0.0) for x, d in zip(kk, st)]
    else:
        kk = [lax.dot_general(d["kb2"].astype(BF16), d["k2"].astype(BF16), nt, preferred_element_type=F32)
              for d in st]
    a = [jnp.where(strict, x[:LANES] * d["decay"], 0.0) for x, d in zip(kk, st)]
    sol = _unit_tri_solve(a, [jnp.concatenate([d["v2"] * d["b2"], d["kb2"] * d["eg"]], axis=1) for d in st],
                          row, col)
    split = lambda x: [jnp.where(is_f, x, 0.0), jnp.where(is_f, 0.0, x)]
    kwu = []
    for d, x in zip(st, sol):
        g2 = d["g2"]
        gl_f, gl_b = g2[DN_CHUNK - 1:DN_CHUNK, :], g2[DN_CHUNK:DN_CHUNK + 1, :]
        gl = jnp.concatenate([jnp.broadcast_to(gl_f, (DN_CHUNK, LANES)),
                              jnp.broadcast_to(gl_b, (DN_CHUNK, LANES))], axis=0)
        kd2 = d["k2"] * jnp.exp(gl - g2)
        d["egl"] = jnp.exp(jnp.concatenate([gl_f, gl_b, jnp.zeros((6, LANES), F32)], axis=0))
        kwu.append(_mm(kd2.T, jnp.concatenate(split(x[:, LANES:]) + split(x[:, :LANES]), axis=1)))
    if with_out:
        awu = [_mm(x, y) for x, y in zip(a_qk, sol)]
    for k, i in enumerate(steps):
        c_s[i, 0] = kwu[k][:, 2 * LANES:3 * LANES]
        c_s[i, 1] = kwu[k][:, 3 * LANES:]
        eg_s[i] = st[k]["egl"]
        lhs_s[i, 0, 0:LANES, :] = (-kwu[k][:, :LANES]).astype(BF16)
        lhs_s[i, 1, 0:LANES, :] = (-kwu[k][:, LANES:2 * LANES]).astype(BF16)
        if with_out:
            q_eff = st[k]["q2"] * st[k]["eg"] - awu[k][:, LANES:]
            o0_s[i] = awu[k][:, :LANES]
            lhs_s[i, 0, LANES:LANES + DN_CHUNK, :] = q_eff[:DN_CHUNK].astype(BF16)
            lhs_s[i, 1, LANES:LANES + DN_CHUNK, :] = q_eff[DN_CHUNK:].astype(BF16)


def _dn_recur(i, n, s_f, s_b, lhs_s, c_s, o0_s, eg_s, o_s, with_out):
    m = LANES + DN_CHUNK if with_out else LANES
    r_f = jnp.dot(lhs_s[i, 0, 0:m, :], s_f.astype(BF16), preferred_element_type=F32)
    r_b = jnp.dot(lhs_s[i, 1, 0:m, :], s_b.astype(BF16), preferred_element_type=F32)
    if with_out:
        o0 = o0_s[i]
        o_s[0, pl.ds(pl.multiple_of(i * DN_CHUNK, DN_CHUNK), DN_CHUNK), :] = r_f[LANES:] + o0[:DN_CHUNK]
        o_s[1, pl.ds(pl.multiple_of((n - 1 - i) * DN_CHUNK, DN_CHUNK), DN_CHUNK), :] = r_b[LANES:] + o0[DN_CHUNK:]
    eg = eg_s[i]
    s_f = s_f * jnp.broadcast_to(eg[0:1, :], (LANES, LANES)) + r_f[:LANES] + c_s[i, 0]
    s_b = s_b * jnp.broadcast_to(eg[1:2, :], (LANES, LANES)) + r_b[:LANES] + c_s[i, 1]
    return s_f, s_b


def _delta_kernel(q_ref, k_ref, v_ref, z_ref, cq_ref, ck_ref, cv_ref, bg_ref, cbg_ref,
                  wq_ref, wk_ref, wv_ref, nw_ref, o_ref,
                  pad_s, qkv_s, sel_s, cqkv_s, csel_s, lhs_s, c_s, o0_s, eg_s, o_s):
    h = pl.program_id(1)
    l = q_ref.shape[2]
    lc = cq_ref.shape[2]
    n, nc = l // DN_CHUNK, lc // DN_CHUNK
    conv_refs = (wq_ref, wk_ref, wv_ref)
    stage = (lhs_s, c_s, o0_s, eg_s)

    _dn_prepare(h, lc, (cq_ref, ck_ref, cv_ref), conv_refs, cbg_ref, pad_s, cqkv_s, csel_s)

    gc = min(DN_GROUP, nc)

    def c_intra(g, c):
        _dn_intra([g * gc + k for k in range(gc)], nc, cqkv_s, csel_s, *stage, with_out=False)
        return c

    lax.fori_loop(0, nc // gc, c_intra, 0)
    zero = jnp.zeros((LANES, LANES), F32)
    s_f, s_b = lax.fori_loop(
        0, nc, lambda i, s: _dn_recur(i, nc, s[0], s[1], *stage, o_s, with_out=False), (zero, zero))

    _dn_prepare(h, l, (q_ref, k_ref, v_ref), conv_refs, bg_ref, pad_s, qkv_s, sel_s)

    def l_intra(g, c):
        _dn_intra([g * DN_GROUP + k for k in range(DN_GROUP)], n, qkv_s, sel_s, *stage, with_out=True)
        return c

    lax.fori_loop(0, n // DN_GROUP, l_intra, 0)
    lax.fori_loop(
        0, n, lambda i, s: _dn_recur(i, n, s[0], s[1], *stage, o_s, with_out=True), (s_f, s_b))

    nw = nw_ref[...]

    def out_tile(t, c):
        rows = pl.ds(pl.multiple_of(t * LANES, LANES), LANES)
        o = o_s[0, rows, :] + o_s[1, rows, :]
        y = o * lax.rsqrt(jnp.mean(o * o, axis=-1, keepdims=True) + EPS) * nw
        o_ref[0, 0, rows, :] = (y * _silu(z_ref[0, 0, rows, :].astype(F32))).astype(o_ref.dtype)
        return c

    lax.fori_loop(0, l // LANES, out_tile, 0)


def _delta_branch(p_lat, p_ctx, bg, cbg, conv_taps, norm_w):
    b, _, l, _ = p_lat.shape
    lc = p_ctx.shape[2]
    n = l // DN_CHUNK
    hb = lambda off: pl.BlockSpec((1, 1, l, LANES), lambda i, j, off=off: (i, off + j, 0, 0))
    cb = lambda off: pl.BlockSpec((1, 1, lc, LANES), lambda i, j, off=off: (i, off + j, 0, 0))
    tb = lambda off: pl.BlockSpec((1, 8, LANES), lambda i, j, off=off: (off + j, 0, 0))
    return pl.pallas_call(
        _delta_kernel,
        grid=(b, N_HEADS),
        in_specs=[hb(0), hb(N_HEADS), hb(2 * N_HEADS), hb(3 * N_HEADS),
                  cb(0), cb(N_HEADS), cb(2 * N_HEADS),
                  pl.BlockSpec((1, l, LANES), lambda i, j: (i, 0, 0)),
                  pl.BlockSpec((1, lc, LANES), lambda i, j: (i, 0, 0)),
                  tb(0), tb(N_HEADS), tb(2 * N_HEADS),
                  pl.BlockSpec((1, LANES), lambda i, j: (0, 0))],
        out_specs=pl.BlockSpec((1, 1, l, LANES), lambda i, j: (i, j, 0, 0)),
        out_shape=jax.ShapeDtypeStruct((b, N_HEADS, l, LANES), BF16),
        scratch_shapes=[pltpu.VMEM((l + 2 * CONV_HALO, LANES), F32),
                        pltpu.VMEM((3, l, LANES), F32),
                        pltpu.VMEM((4, l, LANES), F32),
                        pltpu.VMEM((3, lc, LANES), F32),
                        pltpu.VMEM((4, lc, LANES), F32),
                        pltpu.VMEM((n, 2, LANES + DN_CHUNK, LANES), BF16),
                        pltpu.VMEM((n, 2, LANES, LANES), F32),
                        pltpu.VMEM((n, LANES, LANES), F32),
                        pltpu.VMEM((n, 8, LANES), F32),
                        pltpu.VMEM((2, l, LANES), F32)],
        compiler_params=_cparams("parallel", "arbitrary"),
        name="delta_branch",
    )(p_lat, p_lat, p_lat, p_lat, p_ctx, p_ctx, p_ctx, bg, cbg,
      conv_taps, conv_taps, conv_taps, norm_w.reshape(1, LANES))


def _rope(x, cos, sin):
    lane = lax.broadcasted_iota(jnp.int32, x.shape, 1)
    quarter = HEAD_DIM // 4
    partner = jnp.where((lane // quarter) % 2 == 0,
                        pltpu.roll(x, LANES - quarter, axis=1), pltpu.roll(x, quarter, axis=1))
    return x * cos + partner * sin


def _head_rms(x, w):
    return x * lax.rsqrt(jnp.mean(x * x, axis=-1, keepdims=True) + EPS) * w


def _na_kernel(q_ref, k_ref, v_ref, ck_ref, cv_ref, qnw_ref, knw_ref, cos_ref, sin_ref, bias_ref, o_ref,
               q_s, k_s, ck_s):
    l = q_ref.shape[2]
    rows = l // GRID_W
    kh = min(NA_KH, rows)
    n_loc = kh * GRID_W
    qnw, knw = qnw_ref[...], knw_ref[...]

    def prep(t, c):
        r = pl.ds(pl.multiple_of(t * LANES, LANES), LANES)
        cos, sin = cos_ref[r, :], sin_ref[r, :]
        q = _rope(_head_rms(q_ref[0, 0, r, :].astype(F32), qnw), cos, sin) * (HEAD_DIM ** -0.5)
        q_s[r, :] = q.astype(BF16)
        k_s[r, :] = _rope(_head_rms(k_ref[0, 0, r, :].astype(F32), knw), cos, sin).astype(BF16)
        return c

    lax.fori_loop(0, l // LANES, prep, 0)
    ck_s[...] = _head_rms(ck_ref[0, 0].astype(F32), knw).astype(BF16)
    nt = (((1,), (1,)), ((), ()))

    group = math.gcd(NA_ROW_GROUP, rows)

    def row_group(g, c):
        rr = [g * group + k for k in range(group)]
        rs = [jnp.clip(r - kh // 2, 0, rows - kh) for r in rr]
        qr = [q_s[pl.ds(pl.multiple_of(r * GRID_W, GRID_W), GRID_W), :] for r in rr]
        kloc = [pl.ds(pl.multiple_of(x * GRID_W, GRID_W), n_loc) for x in rs]
        s_loc = [lax.dot_general(q, k_s[kl, :], nt, preferred_element_type=F32) for q, kl in zip(qr, kloc)]
        s_ctx = [lax.dot_general(q, ck_s[...], nt, preferred_element_type=F32) for q in qr]
        s_loc = [s + bias_ref[0, r - x] for s, r, x in zip(s_loc, rr, rs)]
        m = [jnp.maximum(jnp.max(sl, axis=-1, keepdims=True), jnp.max(sc, axis=-1, keepdims=True))
             for sl, sc in zip(s_loc, s_ctx)]
        p_loc = [jnp.exp(s - mi) for s, mi in zip(s_loc, m)]
        p_ctx = [jnp.exp(s - mi) for s, mi in zip(s_ctx, m)]
        denom = [jnp.sum(pl_, axis=-1, keepdims=True) + jnp.sum(pc, axis=-1, keepdims=True)
                 for pl_, pc in zip(p_loc, p_ctx)]
        o_loc = [jnp.dot(p.astype(BF16), v_ref[0, 0, kl, :], preferred_element_type=F32)
                 for p, kl in zip(p_loc, kloc)]
        o_ctx = [jnp.dot(p.astype(BF16), cv_ref[0, 0], preferred_element_type=F32) for p in p_ctx]
        for r, ol, oc, d in zip(rr, o_loc, o_ctx, denom):
            o_ref[0, 0, pl.ds(pl.multiple_of(r * GRID_W, GRID_W), GRID_W), :] = ((ol + oc) / d).astype(o_ref.dtype)
        return c

    lax.fori_loop(0, rows // group, row_group, 0)


def _na_tables(l):
    pos = jnp.arange(l)
    row = (pos // GRID_W).astype(F32)
    col = (pos % GRID_W).astype(F32)
    half = HEAD_DIM // 2
    inv_freq = ROPE_THETA ** (-jnp.arange(0, half, 2, dtype=F32) / half)
    ang_r = row[:, None] * inv_freq[None, :]
    ang_c = col[:, None] * inv_freq[None, :]
    cos = jnp.concatenate([jnp.cos(ang_r), jnp.cos(ang_r), jnp.cos(ang_c), jnp.cos(ang_c)], axis=-1)
    sin = jnp.concatenate([-jnp.sin(ang_r), jnp.sin(ang_r), -jnp.sin(ang_c), jnp.sin(ang_c)], axis=-1)
    return cos, sin


def _na_bias_table(rpb, rows):
    kh = min(NA_KH, rows)
    t = np.arange(NA_KH)[:, None]
    i = np.arange(kh)[None, :]
    dr = np.clip(i - t + NA_KH - 1, 0, 2 * NA_KH - 2)
    q = np.arange(GRID_W)[:, None]
    kc = np.arange(GRID_W)[None, :]
    qstart = np.clip(q - NA_KW // 2, 0, GRID_W - NA_KW)
    in_win = (kc >= qstart) & (kc < qstart + NA_KW)
    dc = np.clip(kc - q + NA_KW - 1, 0, 2 * NA_KW - 2)
    tab = rpb.astype(F32)[:, dr[:, :, None, None], dc[None, None, :, :]]
    tab = jnp.where(jnp.asarray(in_win)[None, None, None], tab, NEG_INF)
    return tab.transpose(0, 1, 3, 2, 4).reshape(rpb.shape[0], NA_KH, GRID_W, kh * GRID_W)


def _na_branch(p_lat, p_ctx, q_norm_w, k_norm_w, rpb):
    b, _, l, _ = p_lat.shape
    lc = p_ctx.shape[2]
    rows = l // GRID_W
    n_loc = min(NA_KH, rows) * GRID_W
    cos, sin = _na_tables(l)
    bias = _na_bias_table(rpb, rows)
    hb = lambda off: pl.BlockSpec((1, 1, l, LANES), lambda j, i, off=off: (i, off + j, 0, 0))
    cb = lambda off: pl.BlockSpec((1, 1, lc, LANES), lambda j, i, off=off: (i, off + j, 0, 0))
    const = lambda shape: pl.BlockSpec(shape, lambda j, i: (0,) * len(shape))
    return pl.pallas_call(
        _na_kernel,
        grid=(N_HEADS, b),
        in_specs=[hb(4 * N_HEADS), hb(5 * N_HEADS), hb(6 * N_HEADS), cb(3 * N_HEADS), cb(4 * N_HEADS),
                  const((1, LANES)), const((1, LANES)), const((l, LANES)), const((l, LANES)),
                  pl.BlockSpec((1, NA_KH, GRID_W, n_loc), lambda j, i: (j, 0, 0, 0))],
        out_specs=pl.BlockSpec((1, 1, l, LANES), lambda j, i: (i, j, 0, 0)),
        out_shape=jax.ShapeDtypeStruct((b, N_HEADS, l, LANES), BF16),
        scratch_shapes=[pltpu.VMEM((l, LANES), BF16), pltpu.VMEM((l, LANES), BF16),
                        pltpu.VMEM((lc, LANES), BF16)],
        compiler_params=_cparams("parallel", "arbitrary"),
        name="na_branch",
    )(p_lat, p_lat, p_lat, p_ctx, p_ctx, q_norm_w.reshape(1, LANES), k_norm_w.reshape(1, LANES),
      cos, sin, bias)


def _merge_kernel(a_ref, b_ref, wa_ref, wb_ref, ga_ref, gb_ref, o_ref, a_s, b_s):
    @pl.when(pl.program_id(1) == 0)
    def _():
        for k in range(N_HEADS):
            a_s[:, k * LANES:(k + 1) * LANES] = a_ref[0, k]
            b_s[:, k * LANES:(k + 1) * LANES] = b_ref[0, k]

    ya = jnp.dot(a_s[...], wa_ref[...], preferred_element_type=F32)
    yb = jnp.dot(b_s[...], wb_ref[...], preferred_element_type=F32)
    for k in range(ga_ref.shape[1]):
        cols = slice(k * LANES, (k + 1) * LANES)
        o_ref[:, cols] = (_sigmoid(ga_ref[0, k].astype(F32)) * ya[:, cols]
                          + _sigmoid(gb_ref[0, k].astype(F32)) * yb[:, cols]).astype(o_ref.dtype)


def _merge(dn_o, na_o, w_a, w_b, p_lat, tm=512, tn=512):
    b, _, l, _ = dn_o.shape
    d = w_a.shape[1]
    tpb = l // tm
    nb = tn // LANES
    head_blk = pl.BlockSpec((1, N_HEADS, tm, LANES), lambda i, j: (i // tpb, 0, i % tpb, 0))
    gate_blk = lambda off: pl.BlockSpec((1, nb, tm, LANES),
                                        lambda i, j, off=off: (i // tpb, off // nb + j, i % tpb, 0))
    w_blk = pl.BlockSpec((N_HEADS * LANES, tn), lambda i, j: (0, j))
    return pl.pallas_call(
        _merge_kernel,
        grid=(b * tpb, d // tn),
        in_specs=[head_blk, head_blk, w_blk, w_blk, gate_blk(7 * N_HEADS), gate_blk(8 * N_HEADS)],
        out_specs=pl.BlockSpec((tm, tn), lambda i, j: (i, j)),
        out_shape=jax.ShapeDtypeStruct((b * l, d), BF16),
        scratch_shapes=[pltpu.VMEM((tm, N_HEADS * LANES), BF16), pltpu.VMEM((tm, N_HEADS * LANES), BF16)],
        compiler_params=_cparams("parallel", "arbitrary"),
        name="merge",
    )(dn_o, na_o, w_a, w_b, p_lat, p_lat)


def _out_kernel(y_ref, w_ref, x_ref, g1_ref, nw_ref, sc_ref, sh_ref, wr_ref, x1_ref, h2_ref, lg_ref):
    x1 = x_ref[...] + g1_ref[0] * jnp.dot(y_ref[...], w_ref[...], preferred_element_type=F32)
    x1_ref[...] = x1
    h2 = (x1 * lax.rsqrt(jnp.mean(x1 * x1, axis=-1, keepdims=True) + EPS) * nw_ref[...]
          * (1.0 + sc_ref[0]) + sh_ref[0])
    h2_ref[...] = h2.astype(BF16)
    lg_ref[...] = jnp.dot(h2, wr_ref[...], preferred_element_type=F32, precision=lax.Precision.HIGHEST)


def _out_proj(y, w_out, x2d, g1, norm_w, scale, shift, w_router_pad, l, tm=256):
    m, d = x2d.shape
    tpb = l // tm
    row_blk = lambda: pl.BlockSpec((tm, d), lambda i: (i, 0))
    mod_blk = lambda: pl.BlockSpec((1, 1, d), lambda i: (i // tpb, 0, 0))
    return pl.pallas_call(
        _out_kernel,
        grid=(m // tm,),
        in_specs=[row_blk(), pl.BlockSpec((d, d), lambda i: (0, 0)), row_blk(), mod_blk(),
                  pl.BlockSpec((1, d), lambda i: (0, 0)), mod_blk(), mod_blk(),
                  pl.BlockSpec((d, LANES), lambda i: (0, 0))],
        out_specs=[row_blk(), row_blk(), pl.BlockSpec((tm, LANES), lambda i: (i, 0))],
        out_shape=[jax.ShapeDtypeStruct((m, d), F32), jax.ShapeDtypeStruct((m, d), BF16),
                   jax.ShapeDtypeStruct((m, LANES), F32)],
        compiler_params=_cparams("parallel"),
        name="out_proj",
    )(y, w_out, x2d, g1, norm_w.reshape(1, d), scale, shift, w_router_pad)


def _route_kernel(lg_ref, slot_t_ref, gate_t_ref, slot_ref, aff_s, slot_s):
    l = lg_ref.shape[0]
    n_tiles = l // LANES
    cap = EC_CAPACITY_FACTOR * l // N_EXPERTS
    lane = lax.broadcasted_iota(jnp.int32, (LANES, LANES), 1)
    row = lax.broadcasted_iota(jnp.int32, (LANES, LANES), 0)

    for t in range(n_tiles):
        x = jnp.where(lane < N_EXPERTS, lg_ref[t * LANES:(t + 1) * LANES, :], -jnp.inf)
        e = jnp.exp(x - jnp.max(x, axis=-1, keepdims=True))
        aff = e / jnp.sum(e, axis=-1, keepdims=True)
        aff_s[:, t * LANES:(t + 1) * LANES] = aff.T[:N_EXPERTS]

    aff_t = aff_s[...]
    keys = pltpu.bitcast(aff_t, jnp.int32)
    count_ge = lambda thr: jnp.sum(jnp.where(keys >= thr, 1.0, 0.0), axis=1, keepdims=True)

    def bisect(_, c):
        lo, hi = c
        mid = lo + (hi - lo) // 2
        ok = count_ge(mid) >= cap
        return jnp.where(ok, mid, lo), jnp.where(ok, hi, mid)

    inf_bits = 0x7F800000
    thr, _ = lax.fori_loop(0, 31, bisect, (jnp.zeros((N_EXPERTS, 1), jnp.int32),
                                           jnp.full((N_EXPERTS, 1), inf_bits, jnp.int32)))
    gt, eq = keys > thr, keys == thr
    need = cap - jnp.sum(jnp.where(gt, 1.0, 0.0), axis=1, keepdims=True)
    before = jnp.where(row < lane, 1.0, 0.0).astype(BF16)

    def excl_prefix(flags):
        out, off = [], jnp.zeros((N_EXPERTS, 1), F32)
        ones = jnp.where(flags, 1.0, 0.0)
        for t in range(n_tiles):
            f = ones[:, t * LANES:(t + 1) * LANES]
            out.append(jnp.dot(f.astype(BF16), before, preferred_element_type=F32) + off)
            off = off + jnp.sum(f, axis=1, keepdims=True)
        return jnp.concatenate(out, axis=1)

    sel = gt | (eq & (excl_prefix(eq) < need))
    slot_f = jnp.where(sel, excl_prefix(sel), -1.0)
    slot_t_ref[0] = slot_f.astype(jnp.int32)
    gate_t_ref[0] = aff_t
    slot_s[...] = jnp.full(slot_s.shape, -1.0, F32)
    slot_s[0:N_EXPERTS, :] = slot_f
    for t in range(n_tiles):
        slot_ref[0, t * LANES:(t + 1) * LANES, :] = slot_s[:, t * LANES:(t + 1) * LANES].T.astype(jnp.int32)


def _route(logits, b, l):
    return pl.pallas_call(
        _route_kernel,
        grid=(b,),
        in_specs=[pl.BlockSpec((l, LANES), lambda i: (i, 0))],
        out_specs=[pl.BlockSpec((1, N_EXPERTS, l), lambda i: (i, 0, 0)),
                   pl.BlockSpec((1, N_EXPERTS, l), lambda i: (i, 0, 0)),
                   pl.BlockSpec((1, l, LANES), lambda i: (i, 0, 0))],
        out_shape=[jax.ShapeDtypeStruct((b, N_EXPERTS, l), jnp.int32),
                   jax.ShapeDtypeStruct((b, N_EXPERTS, l), F32),
                   jax.ShapeDtypeStruct((b, l, LANES), jnp.int32)],
        scratch_shapes=[pltpu.VMEM((N_EXPERTS, l), F32), pltpu.VMEM((LANES, l), F32)],
        compiler_params=_cparams("parallel"),
        name="route",
    )(logits)


def _ffn_kernel(h_ref, slot_t_ref, gate_t_ref, w1_ref, w3_ref, w2_ref, o_ref):
    e = pl.program_id(0)
    cap, l = o_ref.shape[2], h_ref.shape[1]
    hit = lax.broadcasted_iota(jnp.int32, (cap, l), 0) == slot_t_ref[0, pl.ds(e, 1), :]
    gate = jnp.sum(jnp.where(hit, gate_t_ref[0, pl.ds(e, 1), :], 0.0), axis=1, keepdims=True)
    x = jnp.dot(jnp.where(hit, 1.0, 0.0).astype(BF16), h_ref[0], preferred_element_type=F32).astype(BF16)
    h1 = jnp.dot(x, w1_ref[0], preferred_element_type=F32)
    h3 = jnp.dot(x, w3_ref[0], preferred_element_type=F32)
    hid = (_silu(h1) * h3).astype(BF16)
    o_ref[0, 0] = (jnp.dot(hid, w2_ref[0], preferred_element_type=F32) * gate).astype(o_ref.dtype)


def _expert_ffn(h2, slot_t, gate_t, w1, w3, w2, cap):
    b, l, d = h2.shape
    e, _, f = w1.shape
    return pl.pallas_call(
        _ffn_kernel,
        grid=(e, b),
        in_specs=[pl.BlockSpec((1, l, d), lambda j, i: (i, 0, 0)),
                  pl.BlockSpec((1, e, l), lambda j, i: (i, 0, 0)),
                  pl.BlockSpec((1, e, l), lambda j, i: (i, 0, 0)),
                  pl.BlockSpec((1, d, f), lambda j, i: (j, 0, 0)),
                  pl.BlockSpec((1, d, f), lambda j, i: (j, 0, 0)),
                  pl.BlockSpec((1, f, d), lambda j, i: (j, 0, 0))],
        out_specs=pl.BlockSpec((1, 1, cap, d), lambda j, i: (i, j, 0, 0)),
        out_shape=jax.ShapeDtypeStruct((b, e, cap, d), BF16),
        compiler_params=_cparams("parallel", "arbitrary"),
        name="expert_ffn",
    )(h2, slot_t, gate_t, w1, w3, w2)


def _combine_kernel(slot_ref, y_ref, x1_ref, g2_ref, o_ref):
    tm = slot_ref.shape[1]
    cap = y_ref.shape[1] // N_EXPERTS
    slot = slot_ref[0]
    j = lax.broadcasted_iota(jnp.int32, (tm, cap), 1)
    onehot = jnp.concatenate([jnp.where(slot[:, e:e + 1] == j, 1.0, 0.0).astype(BF16) for e in range(N_EXPERTS)],
                             axis=1)
    o_ref[0] = x1_ref[0] + g2_ref[0] * jnp.dot(onehot, y_ref[0], preferred_element_type=F32)


def _combine(slot, ye, x1, g2, tm=512, tn=1024):
    b, l, d = x1.shape
    ec = ye.shape[1]
    return pl.pallas_call(
        _combine_kernel,
        grid=(b, d // tn, l // tm),
        in_specs=[pl.BlockSpec((1, tm, LANES), lambda i, n, m: (i, m, 0)),
                  pl.BlockSpec((1, ec, tn), lambda i, n, m: (i, 0, n)),
                  pl.BlockSpec((1, tm, tn), lambda i, n, m: (i, m, n)),
                  pl.BlockSpec((1, 1, tn), lambda i, n, m: (i, 0, n))],
        out_specs=pl.BlockSpec((1, tm, tn), lambda i, n, m: (i, m, n)),
        out_shape=jax.ShapeDtypeStruct((b, l, d), F32),
        compiler_params=_cparams("parallel", "parallel", "arbitrary"),
        name="moe_combine",
    )(slot, ye, x1, g2)


def _layer(x, ctx, mod, mod_c, norm1_w, w_in, conv_w, a_log, dt_bias, dn_norm_w, q_norm_w, k_norm_w, rpb,
           w_a, w_b, w_out, norm2_w, w_router, w1, w3, w2):
    b, l, d = x.shape
    sh1, sc1, g1, sh2, sc2, g2 = [m[:, None, :] for m in jnp.split(mod, 6, axis=-1)]
    sh1c, sc1c = mod_c[None, None, :d], mod_c[None, None, d:2 * d]

    hd = N_HEADS * HEAD_DIM
    offs = np.cumsum([0, hd, hd, hd, hd, 2 * N_HEADS, 2 * N_HEADS, hd, hd, hd, d, d])
    col = lambda k: w_in[:, offs[k]:offs[k + 1]]
    w_main = jnp.concatenate([col(k) for k in (0, 1, 2, 3, 6, 7, 8, 9, 10)], axis=1).astype(BF16)
    w_ctx = jnp.concatenate([col(k) for k in (0, 1, 2, 7, 8)], axis=1).astype(BF16)
    w_ba = jnp.pad(jnp.concatenate([col(4), col(5)], axis=1), ((0, 0), (0, LANES - 4 * N_HEADS))).astype(BF16)

    p_lat = _norm_proj(x, norm1_w, sc1, sh1, w_main, BF16, tm=1024, tn=512)
    p_ctx = _norm_proj(ctx, norm1_w, sc1c, sh1c, w_ctx, BF16, tm=ctx.shape[1], tn=512)
    ba = _norm_proj(x, norm1_w, sc1, sh1, w_ba, F32, tm=1024, tn=LANES).reshape(b, l, LANES)
    cba = _norm_proj(ctx, norm1_w, sc1c, sh1c, w_ba, F32, tm=ctx.shape[1], tn=LANES).reshape(
        b, ctx.shape[1], LANES)

    lanes_pad = lambda v: jnp.pad(v.reshape(1, -1).astype(F32), ((0, 0), (2 * N_HEADS, LANES - 4 * N_HEADS)))
    alog_l, dtb_l = lanes_pad(a_log), lanes_pad(dt_bias)
    bg, cbg = _dn_gates(ba, alog_l, dtb_l), _dn_gates(cba, alog_l, dtb_l)
    taps = jnp.pad(conv_w.astype(F32), ((0, 8 - CONV_K), (0, 0))).reshape(8, 3 * N_HEADS, LANES).transpose(1, 0, 2)
    dn_o = _delta_branch(p_lat, p_ctx, bg, cbg, taps, dn_norm_w.astype(F32))
    na_o = _na_branch(p_lat, p_ctx, q_norm_w.astype(F32), k_norm_w.astype(F32), rpb)

    y = _merge(dn_o, na_o, w_a.astype(BF16), w_b.astype(BF16), p_lat)
    w_router_pad = jnp.pad(w_router.astype(F32), ((0, 0), (0, LANES - N_EXPERTS)))
    x1, h2, logits = _out_proj(y, w_out.astype(BF16), x.reshape(b * l, d), g1, norm2_w, sc2, sh2,
                               w_router_pad, l)

    cap = EC_CAPACITY_FACTOR * l // N_EXPERTS
    slot_t, gate_t, slot = _route(logits, b, l)
    ye = _expert_ffn(h2.reshape(b, l, d), slot_t, gate_t, w1.astype(BF16), w3.astype(BF16), w2.astype(BF16), cap)
    return _combine(slot, ye.reshape(b, N_EXPERTS * cap, d), x1.reshape(b, l, d), g2)


def kernel(x, c, ctx, c_ctx, ada_w, ada_b, norm1_w, w_in, conv_w, dn_a_log, dn_dt_bias, dn_norm_w,
           na_q_norm_w, na_k_norm_w, na_rpb, w_branch_a, w_branch_b, w_out, norm2_w, w_router,
           expert_w1, expert_w3, expert_w2):
    b = x.shape[0]
    depth = ada_w.shape[0]
    cvec = jnp.concatenate([c, c_ctx[None, :], jnp.zeros((16 - b - 1, c.shape[1]), c.dtype)], axis=0)
    for i in range(depth):
        mod_all = _adaln_mod(cvec, ada_w[i], ada_b[i])
        x = _layer(x, ctx, mod_all[:b], mod_all[b], norm1_w[i], w_in[i], conv_w[i], dn_a_log[i],
                   dn_dt_bias[i], dn_norm_w[i], na_q_norm_w[i], na_k_norm_w[i], na_rpb[i],
                   w_branch_a[i], w_branch_b[i], w_out[i], norm2_w[i], w_router[i],
                   expert_w1[i], expert_w3[i], expert_w2[i])
    return x
```

```python
import functools
import math

import numpy as np
import jax
import jax.numpy as jnp
from jax import lax
from jax.experimental import pallas as pl
from jax.experimental.pallas import tpu as pltpu

F32 = jnp.float32
BF16 = jnp.bfloat16

EPS = 1e-6
NEG_INF = -1e30
LANES = 128
GRID_W = 64
N_HEADS = 16
HEAD_DIM = 128
CONV_K = 5
CONV_HALO = 16
DN_CHUNK = 64
DN_GROUP = 8
DN_SOLVE_BASE = 16
NA_KH = 8
NA_KW = 16
NA_ROW_GROUP = 8
ROPE_THETA = 10000.0
N_EXPERTS = 16
EC_CAPACITY_FACTOR = 2
VMEM_LIMIT = 56 * 1024 * 1024


def _cparams(*sem):
    return pltpu.CompilerParams(dimension_semantics=sem, vmem_limit_bytes=VMEM_LIMIT)


def _sigmoid(x):
    return 1.0 / (1.0 + jnp.exp(-x))


def _silu(x):
    return x * _sigmoid(x)


def _mod_kernel(c_ref, w_ref, b_ref, o_ref):
    s = _silu(c_ref[...])
    o_ref[...] = jnp.dot(s, w_ref[...], preferred_element_type=F32,
                         precision=lax.Precision.HIGHEST) + b_ref[...]


def _adaln_mod(cvec, ada_w, ada_b, tn=1024):
    m, d = cvec.shape
    n = ada_w.shape[1]
    return pl.pallas_call(
        _mod_kernel,
        grid=(n // tn,),
        in_specs=[pl.BlockSpec((m, d), lambda j: (0, 0)),
                  pl.BlockSpec((d, tn), lambda j: (0, j)),
                  pl.BlockSpec((1, tn), lambda j: (0, j))],
        out_specs=pl.BlockSpec((m, tn), lambda j: (0, j)),
        out_shape=jax.ShapeDtypeStruct((m, n), F32),
        compiler_params=_cparams("parallel"),
        name="adaln_mod",
    )(cvec, ada_w, ada_b.reshape(1, n))


def _norm_proj_kernel(x_ref, nw_ref, sc_ref, sh_ref, w_ref, wg_ref, o_ref, og_ref, h_ref):
    n_seq, _, rows, _ = o_ref.shape

    @pl.when(pl.program_id(1) == 0)
    def _():
        x = x_ref[...]
        y = x * lax.rsqrt(jnp.mean(x * x, axis=-1, keepdims=True) + EPS) * nw_ref[...]
        h_ref[...] = (y * (1.0 + sc_ref[0]) + sh_ref[0]).astype(BF16)
        og = jnp.dot(h_ref[...], wg_ref[...], preferred_element_type=F32)
        for s in range(n_seq):
            og_ref[s] = og[s * rows:(s + 1) * rows]

    acc =jnp.dot(h_ref[...], w_ref[...], preferred_element_type=F32)
    for s in range(n_seq):
        for k in range(o_ref.shape[1]):
            o_ref[s, k] = acc[s * rows:(s + 1) * rows, k * LANES:(k + 1) * LANES].astype(o_ref.dtype)


def _norm_proj(x, norm_w, scale, shift, w, w_gates, tm, tn):
    b, l, d = x.shape
    n = w.shape[1]
    tpb = max(l // tm, 1)
    spt = max(tm // l, 1)
    per_sample = scale.shape[0] != 1
    assert spt == 1 or not per_sample
    mod_idx = (lambda i, j: (i // tpb, 0, 0)) if per_sample else (lambda i, j: (0, 0, 0))
    return pl.pallas_call(
        _norm_proj_kernel,
        grid=(b * l // tm, n // tn),
        in_specs=[pl.BlockSpec((tm, d), lambda i, j: (i, 0)),
                  pl.BlockSpec((1, d), lambda i, j: (0, 0)),
                  pl.BlockSpec((1, 1, d), mod_idx),
                  pl.BlockSpec((1, 1, d), mod_idx),
                  pl.BlockSpec((d, tn), lambda i, j: (0, j)),
                  pl.BlockSpec((d, LANES), lambda i, j: (0, 0))],
        out_specs=[pl.BlockSpec((spt, tn // LANES, tm // spt, LANES), lambda i, j: (i // tpb, j, i % tpb, 0)),
                   pl.BlockSpec((spt, tm // spt, LANES), lambda i, j: (i // tpb, i % tpb, 0))],
        out_shape=[jax.ShapeDtypeStruct((b, n // LANES, l, LANES), BF16),
                   jax.ShapeDtypeStruct((b, l, LANES), F32)],
        scratch_shapes=[pltpu.VMEM((tm, d), BF16)],
        compiler_params=_cparams("parallel", "arbitrary"),
        name="norm_proj",
    )(x.reshape(b * l, d), norm_w.reshape(1, d), scale, shift, w, w_gates)


def _dn_gates_kernel(x_ref, alog_ref, dtb_ref, o_ref):
    l = x_ref.shape[1]
    lane = lax.broadcasted_iota(jnp.int32, (LANES, LANES), 1)
    row = lax.broadcasted_iota(jnp.int32, (LANES, LANES), 0)
    same_chunk = (row // DN_CHUNK) == (lane // DN_CHUNK)
    prefix_m = jnp.where(same_chunk & (lane <= row), 1.0, 0.0).astype(F32)
    suffix_m = jnp.where(same_chunk & (lane >= row), 1.0, 0.0).astype(F32)
    neg_a = -jnp.exp(alog_ref[...])
    dtb = dtb_ref[...]

    def tile(t, carry):
        rows = pl.ds(pl.multiple_of(t * LANES, LANES), LANES)
        x = x_ref[0, rows, :]
        beta = _sigmoid(x)
        z = x + dtb
        g = neg_a * (jnp.maximum(z, 0.0) + jnp.log1p(jnp.exp(-jnp.abs(z))))
        pre = jnp.dot(prefix_m, g, preferred_element_type=F32, precision=lax.Precision.HIGHEST)
        suf = jnp.dot(suffix_m, g, preferred_element_type=F32, precision=lax.Precision.HIGHEST)
        o_ref[0, rows, :] = jnp.where(lane < 2 * N_HEADS, beta, jnp.where(lane < 3 * N_HEADS, pre, suf))
        return carry

    lax.fori_loop(0, l // LANES, tile, 0)


def _dn_gates(ba, alog_lanes, dtb_lanes):
    b, l, _ = ba.shape
    return pl.pallas_call(
        _dn_gates_kernel,
        grid=(b,),
        in_specs=[pl.BlockSpec((1, l, LANES), lambda i: (i, 0, 0)),
                  pl.BlockSpec((1, LANES), lambda i: (0, 0)),
                  pl.BlockSpec((1, LANES), lambda i: (0, 0))],
        out_specs=pl.BlockSpec((1, l, LANES), lambda i: (i, 0, 0)),
        out_shape=jax.ShapeDtypeStruct((b, l, LANES), F32),
        compiler_params=_cparams("parallel"),
        name="dn_gates",
    )(ba, alog_lanes, dtb_lanes)


def _dn_masks():
    row = lax.broadcasted_iota(jnp.int32, (LANES, LANES), 0)
    col = lax.broadcasted_iota(jnp.int32, (LANES, LANES), 1)
    fwd = row < DN_CHUNK
    same = (row // DN_CHUNK) == (col // DN_CHUNK)
    tril = same & ((fwd & (row >= col)) | (~fwd & (row <= col)))
    strict = tril & (row != col)
    return row, col, tril, strict


def _dn_prepare(h, n_rows, raw_refs, conv_refs, bg_ref, pad_ref, qkv_s, sel_s):
    n_tiles = n_rows // LANES
    zeros_halo = jnp.zeros((CONV_HALO, LANES), F32)
    for which in range(3):
        pad_ref[which, 0:CONV_HALO, :] = zeros_halo
        pad_ref[which, CONV_HALO + n_rows:2 * CONV_HALO + n_rows, :] = zeros_halo
        pad_ref[which, CONV_HALO:CONV_HALO + n_rows, :] = raw_refs[which][0, 0].astype(F32)
    taps = [conv_refs[which][0] for which in range(3)]

    def tile(t, carry):
        rows = pl.ds(pl.multiple_of(t * LANES, LANES), LANES)
        base = t * LANES + CONV_HALO - CONV_K // 2
        ys = []
        for which in range(3):
            acc = pad_ref[which, pl.ds(base, LANES), :] * taps[which][0:1, :]
            for j in range(1, CONV_K):
                acc = acc + pad_ref[which, pl.ds(base + j, LANES), :] * taps[which][j:j + 1, :]
            ys.append(_silu(acc))
        q, k, v = ys
        qkv_s[0, rows, :] = q * (lax.rsqrt(jnp.sum(q * q, axis=-1, keepdims=True) + EPS) * (HEAD_DIM ** -0.5))
        qkv_s[1, rows, :] = k * lax.rsqrt(jnp.sum(k * k, axis=-1, keepdims=True) + EPS)
        qkv_s[2, rows, :] = v
        x = pltpu.roll(bg_ref[0, rows, :], (LANES - h) % LANES, axis=1)
        for s in range(4):
            sel_s[s, rows, :] = jnp.broadcast_to(x[:, s * N_HEADS:s * N_HEADS + 1], x.shape)
        return carry

    lax.fori_loop(0, n_tiles, tile, 0, unroll=2)


def _mm(x, y):
    return jnp.dot(x.astype(BF16), y.astype(BF16), preferred_element_type=F32)


def _unit_tri_solve(a_all, rhs_all, row, col):
    blk = lambda m: (row // m) == (col // m)
    eye = jnp.where(row == col, 1.0, 0.0)
    ad = [jnp.where(blk(DN_SOLVE_BASE), a, 0.0) for a in a_all]
    t = [eye - x for x in ad]
    p = [_mm(x, x) for x in ad]
    for _ in range(int(math.log2(DN_SOLVE_BASE)) - 2):
        tp = [_mm(jnp.concatenate([ti, pi], axis=0), pi) for ti, pi in zip(t, p)]
        t = [ti + x[:LANES] for ti, x in zip(t, tp)]
        p = [x[LANES:] for x in tp]
    tp = [_mm(ti, pi) for ti, pi in zip(t, p)]
    t = [ti + x for ti, x in zip(t, tp)]
    k = 2 * DN_SOLVE_BASE
    while k < DN_CHUNK:
        nk = [_mm(ti, jnp.where(blk(k) & ~blk(k // 2), a, 0.0)) for ti, a in zip(t, a_all)]
        tn = [_mm(x, ti) for x, ti in zip(nk, t)]
        t = [ti - x for ti, x in zip(t, tn)]
        k *= 2
    ny = [_mm(ti, jnp.concatenate([jnp.where(blk(k // 2), 0.0, a), r], axis=1))
          for ti, a, r in zip(t, a_all, rhs_all)]
    ny2 = [_mm(x[:, :LANES], x[:, LANES:]) for x in ny]
    return [x[:, LANES:] - z for x, z in zip(ny, ny2)]


def _dn_intra(steps, n, qkv_s, sel_s, lhs_s, c_s, o0_s, eg_s, with_out):
    row, col, tril, strict = _dn_masks()
    is_f = row < DN_CHUNK
    nt = (((1,), (1,)), ((), ()))

    def load(i):
        ri = pl.ds(pl.multiple_of(i * DN_CHUNK, DN_CHUNK), DN_CHUNK)
        rj = pl.ds(pl.multiple_of((n - 1 - i) * DN_CHUNK, DN_CHUNK), DN_CHUNK)
        pair = lambda ref, a, b: jnp.concatenate([ref[a, ri, :], ref[b, rj, :]], axis=0)
        q2, k2, v2 = pair(qkv_s, 0, 0), pair(qkv_s, 1, 1), pair(qkv_s, 2, 2)
        b2, g2 = pair(sel_s, 0, 1), pair(sel_s, 2, 3)
        decay = jnp.exp(jnp.where(tril, g2 - g2.T, -jnp.inf))
        return dict(q2=q2, k2=k2, v2=v2, b2=b2, g2=g2, decay=decay, kb2=k2 * b2, eg=jnp.exp(g2))

    st = [load(i) for i in steps]
    if with_out:
        kk = [lax.dot_general(jnp.concatenate([d["kb2"], d["q2"]], axis=0).astype(BF16), d["k2"].astype(BF16), nt,
                              preferred_element_type=F32) for d in st]
        a_qk = [jnp.where(tril, x[LANES:] * d["decay"], 0.0) for x, d in zip(kk, st)]
    else:
        kk = [lax.dot_general(d["kb2"].astype(BF16), d["k2"].astype(BF16), nt, preferred_element_type=F32)
              for d in st]
    a = [jnp.where(strict, x[:LANES] * d["decay"], 0.0) for x, d in zip(kk, st)]
    sol = _unit_tri_solve(a, [jnp.concatenate([d["v2"] * d["b2"], d["kb2"] * d["eg"]], axis=1) for d in st],
                          row, col)
    split = lambda x: [jnp.where(is_f, x, 0.0), jnp.where(is_f, 0.0, x)]
    kwu = []
    for d, x in zip(st, sol):
        g2 = d["g2"]
        gl_f, gl_b = g2[DN_CHUNK - 1:DN_CHUNK, :], g2[DN_CHUNK:DN_CHUNK + 1, :]
        gl = jnp.concatenate([jnp.broadcast_to(gl_f, (DN_CHUNK, LANES)),
                              jnp.broadcast_to(gl_b, (DN_CHUNK, LANES))], axis=0)
        kd2 = d["k2"] * jnp.exp(gl - g2)
        d["egl"] = jnp.exp(jnp.concatenate([gl_f, gl_b, jnp.zeros((6, LANES), F32)], axis=0))
        kwu.append(_mm(kd2.T, jnp.concatenate(split(x[:, LANES:]) + split(x[:, :LANES]), axis=1)))
    if with_out:
        awu = [_mm(x, y) for x, y in zip(a_qk, sol)]
    for k, i in enumerate(steps):
        c_s[i, 0] = kwu[k][:, 2 * LANES:3 * LANES]
        c_s[i, 1] = kwu[k][:, 3 * LANES:]
        eg_s[i] = st[k]["egl"]
        lhs_s[i, 0, 0:LANES, :] = (-kwu[k][:, :LANES]).astype(BF16)
        lhs_s[i, 1, 0:LANES, :] = (-kwu[k][:, LANES:2 * LANES]).astype(BF16)
        if with_out:
            q_eff = st[k]["q2"] * st[k]["eg"] - awu[k][:, LANES:]
            o0_s[i] = awu[k][:, :LANES]
            lhs_s[i, 0, LANES:LANES + DN_CHUNK, :] = q_eff[:DN_CHUNK].astype(BF16)
            lhs_s[i, 1, LANES:LANES + DN_CHUNK, :] = q_eff[DN_CHUNK:].astype(BF16)


def _dn_recur(i, n, s_f, s_b, lhs_s, c_s, o0_s, eg_s, o_s, with_out):
    m = LANES + DN_CHUNK if with_out else LANES
    r_f = jnp.dot(lhs_s[i, 0, 0:m, :], s_f.astype(BF16), preferred_element_type=F32)
    r_b = jnp.dot(lhs_s[i, 1, 0:m, :], s_b.astype(BF16), preferred_element_type=F32)
    if with_out:
        o0 = o0_s[i]
        o_s[0, pl.ds(pl.multiple_of(i * DN_CHUNK, DN_CHUNK), DN_CHUNK), :] = r_f[LANES:] + o0[:DN_CHUNK]
        o_s[1, pl.ds(pl.multiple_of((n - 1 - i) * DN_CHUNK, DN_CHUNK), DN_CHUNK), :] = r_b[LANES:] + o0[DN_CHUNK:]
    eg = eg_s[i]
    s_f = s_f * jnp.broadcast_to(eg[0:1, :], (LANES, LANES)) + r_f[:LANES] + c_s[i, 0]
    s_b = s_b * jnp.broadcast_to(eg[1:2, :], (LANES, LANES)) + r_b[:LANES] + c_s[i, 1]
    return s_f, s_b


def _delta_kernel(q_ref, k_ref, v_ref, z_ref, cq_ref, ck_ref, cv_ref, bg_ref, cbg_ref,
                  wq_ref, wk_ref, wv_ref, nw_ref, o_ref,
                  pad_s, qkv_s, sel_s, cqkv_s, csel_s, lhs_s, c_s, o0_s, eg_s, o_s):
    h = pl.program_id(1)
    l = q_ref.shape[2]
    lc = cq_ref.shape[2]
    n, nc = l // DN_CHUNK, lc // DN_CHUNK
    conv_refs = (wq_ref, wk_ref, wv_ref)
    stage = (lhs_s, c_s, o0_s, eg_s)

    _dn_prepare(h, lc, (cq_ref, ck_ref, cv_ref), conv_refs, cbg_ref, pad_s, cqkv_s, csel_s)

    gc = min(DN_GROUP, nc)

    def c_intra(g, c):
        _dn_intra([g * gc + k for k in range(gc)], nc, cqkv_s, csel_s, *stage, with_out=False)
        return c

    lax.fori_loop(0, nc // gc, c_intra, 0)
    zero = jnp.zeros((LANES, LANES), F32)
    s_f, s_b = lax.fori_loop(
        0, nc, lambda i, s: _dn_recur(i, nc, s[0], s[1], *stage, o_s, with_out=False), (zero, zero))

    _dn_prepare(h, l, (q_ref, k_ref, v_ref), conv_refs, bg_ref, pad_s, qkv_s, sel_s)

    def l_intra(g, c):
        _dn_intra([g * DN_GROUP + k for k in range(DN_GROUP)], n, qkv_s, sel_s, *stage, with_out=True)
        return c

    lax.fori_loop(0, n // DN_GROUP, l_intra, 0)
    lax.fori_loop(
        0, n, lambda i, s: _dn_recur(i, n, s[0], s[1], *stage, o_s, with_out=True), (s_f, s_b))

    nw = nw_ref[...]

    def out_tile(t, c):
        rows = pl.ds(pl.multiple_of(t * LANES, LANES), LANES)
        o = o_s[0, rows, :] + o_s[1, rows, :]
        y = o * lax.rsqrt(jnp.mean(o * o, axis=-1, keepdims=True) + EPS) * nw
        o_ref[0, 0, rows, :] = (y * _silu(z_ref[0, 0, rows, :].astype(F32))).astype(o_ref.dtype)
        return c

    lax.fori_loop(0, l // LANES, out_tile, 0)


def _delta_branch(p_lat, p_ctx, bg, cbg, conv_taps, norm_w):
    b, _, l, _ = p_lat.shape
    lc = p_ctx.shape[2]
    n = l // DN_CHUNK
    hb = lambda off: pl.BlockSpec((1, 1, l, LANES), lambda i, j, off=off: (i, off + j, 0, 0))
    cb = lambda off: pl.BlockSpec((1, 1, lc, LANES), lambda i, j, off=off: (i, off + j, 0, 0))
    tb = lambda off: pl.BlockSpec((1, 8, LANES), lambda i, j, off=off: (off + j, 0, 0))
    return pl.pallas_call(
        _delta_kernel,
        grid=(b, N_HEADS),
        in_specs=[hb(0), hb(N_HEADS), hb(2 * N_HEADS), hb(3 * N_HEADS),
                  cb(0), cb(N_HEADS), cb(2 * N_HEADS),
                  pl.BlockSpec((1, l, LANES), lambda i, j: (i, 0, 0)),
                  pl.BlockSpec((1, lc, LANES), lambda i, j: (i, 0, 0)),
                  tb(0), tb(N_HEADS), tb(2 * N_HEADS),
                  pl.BlockSpec((1, LANES), lambda i, j: (0, 0))],
        out_specs=pl.BlockSpec((1, 1, l, LANES), lambda i, j: (i, j, 0, 0)),
        out_shape=jax.ShapeDtypeStruct((b, N_HEADS, l, LANES), BF16),
        scratch_shapes=[pltpu.VMEM((3, l + 2 * CONV_HALO, LANES), F32),
                        pltpu.VMEM((3, l, LANES), F32),
                        pltpu.VMEM((4, l, LANES), F32),
                        pltpu.VMEM((3, lc, LANES), F32),
                        pltpu.VMEM((4, lc, LANES), F32),
                        pltpu.VMEM((n, 2, LANES + DN_CHUNK, LANES), BF16),
                        pltpu.VMEM((n, 2, LANES, LANES), F32),
                        pltpu.VMEM((n, LANES, LANES), F32),
                        pltpu.VMEM((n, 8, LANES), F32),
                        pltpu.VMEM((2, l, LANES), F32)],
        compiler_params=_cparams("parallel", "arbitrary"),
        name="delta_branch",
    )(p_lat, p_lat, p_lat, p_lat, p_ctx, p_ctx, p_ctx, bg, cbg,
      conv_taps, conv_taps, conv_taps, norm_w.reshape(1, LANES))


def _rope_perm():
    quarter = HEAD_DIM // 4
    return np.concatenate([np.arange(quarter), 2 * quarter + np.arange(quarter),
                           quarter + np.arange(quarter), 3 * quarter + np.arange(quarter)])


def _rope(x, cos, sin):
    return x * cos + pltpu.roll(x, HEAD_DIM // 2, axis=1) * sin


def _head_rms(x, w):
    return x * lax.rsqrt(jnp.mean(x * x, axis=-1, keepdims=True) + EPS) * w


def _na_kernel(q_ref, k_ref, v_ref, ck_ref, cv_ref, qnw_ref, knw_ref, cos_ref, sin_ref, bias_ref, o_ref,
               q_s, k_s, ck_s):
    l = q_ref.shape[2]
    rows = l // GRID_W
    kh = min(NA_KH, rows)
    n_loc = kh * GRID_W
    qnw, knw = qnw_ref[...], knw_ref[...]

    def prep(t, c):
        r = pl.ds(pl.multiple_of(t * LANES, LANES), LANES)
        cos, sin = cos_ref[r, :], sin_ref[r, :]
        q = _rope(_head_rms(q_ref[0, 0, r, :].astype(F32), qnw), cos, sin) * (HEAD_DIM ** -0.5)
        q_s[r, :] = q.astype(BF16)
        k_s[r, :] = _rope(_head_rms(k_ref[0, 0, r, :].astype(F32), knw), cos, sin).astype(BF16)
        return c

    lax.fori_loop(0, l // LANES, prep, 0, unroll=2)
    ck_s[...] = _head_rms(ck_ref[0, 0].astype(F32), knw).astype(BF16)
    nt = (((1,), (1,)), ((), ()))

    group = math.gcd(NA_ROW_GROUP, rows)

    def row_group(g, c):
        rr = [g * group + k for k in range(group)]
        rs = [jnp.clip(r - kh // 2, 0, rows - kh) for r in rr]
        qr = [q_s[pl.ds(pl.multiple_of(r * GRID_W, GRID_W), GRID_W), :] for r in rr]
        kloc = [pl.ds(pl.multiple_of(x * GRID_W, GRID_W), n_loc) for x in rs]
        s_loc = [lax.dot_general(q, k_s[kl, :], nt, preferred_element_type=F32) for q, kl in zip(qr, kloc)]
        s_ctx = [lax.dot_general(q, ck_s[...], nt, preferred_element_type=F32) for q in qr]
        s_loc = [s + bias_ref[0, r - x] for s, r, x in zip(s_loc, rr, rs)]
        m = [jnp.maximum(jnp.max(sl, axis=-1, keepdims=True), jnp.max(sc, axis=-1, keepdims=True))
             for sl, sc in zip(s_loc, s_ctx)]
        p_loc = [jnp.exp(s - mi) for s, mi in zip(s_loc, m)]
        p_ctx = [jnp.exp(s - mi) for s, mi in zip(s_ctx, m)]
        denom = [jnp.sum(pl_, axis=-1, keepdims=True) + jnp.sum(pc, axis=-1, keepdims=True)
                 for pl_, pc in zip(p_loc, p_ctx)]
        o_loc = [jnp.dot(p.astype(BF16), v_ref[0, 0, kl, :], preferred_element_type=F32)
                 for p, kl in zip(p_loc, kloc)]
        o_ctx = [jnp.dot(p.astype(BF16), cv_ref[0, 0], preferred_element_type=F32) for p in p_ctx]
        for r, ol, oc, d in zip(rr, o_loc, o_ctx, denom):
            o_ref[0, 0, pl.ds(pl.multiple_of(r * GRID_W, GRID_W), GRID_W), :] = ((ol + oc) / d).astype(o_ref.dtype)
        return c

    lax.fori_loop(0, rows // group, row_group, 0)


def _na_tables(l):
    pos = jnp.arange(l)
    row = (pos // GRID_W).astype(F32)
    col = (pos % GRID_W).astype(F32)
    half = HEAD_DIM // 2
    inv_freq = ROPE_THETA ** (-jnp.arange(0, half, 2, dtype=F32) / half)
    ang_r = row[:, None] * inv_freq[None, :]
    ang_c = col[:, None] * inv_freq[None, :]
    cos = jnp.concatenate([jnp.cos(ang_r), jnp.cos(ang_r), jnp.cos(ang_c), jnp.cos(ang_c)], axis=-1)
    sin = jnp.concatenate([-jnp.sin(ang_r), jnp.sin(ang_r), -jnp.sin(ang_c), jnp.sin(ang_c)], axis=-1)
    perm = _rope_perm()
    return cos[:, perm], sin[:, perm]


def _na_bias_table(rpb, rows):
    kh = min(NA_KH, rows)
    t = np.arange(NA_KH)[:, None]
    i = np.arange(kh)[None, :]
    dr = np.clip(i - t + NA_KH - 1, 0, 2 * NA_KH - 2)
    q = np.arange(GRID_W)[:, None]
    kc = np.arange(GRID_W)[None, :]
    qstart = np.clip(q - NA_KW // 2, 0, GRID_W - NA_KW)
    in_win = (kc >= qstart) & (kc < qstart + NA_KW)
    dc = np.clip(kc - q + NA_KW - 1, 0, 2 * NA_KW - 2)
    pick_r = (dr[:, :, None] == np.arange(2 * NA_KH - 1)).astype(np.float32)
    pick_c = (np.arange(2 * NA_KW - 1)[:, None, None] == dc[None]).astype(np.float32)
    tab = jnp.einsum("tir,hrc,cqk->htqik", pick_r, rpb.astype(F32), pick_c, precision=lax.Precision.HIGHEST)
    tab = jnp.where(jnp.asarray(in_win)[None, None, :, None, :], tab, NEG_INF)
    return tab.reshape(rpb.shape[0], NA_KH, GRID_W, kh * GRID_W)


def _na_branch(p_lat, p_ctx, q_norm_w, k_norm_w, rpb):
    b, _, l, _ = p_lat.shape
    lc = p_ctx.shape[2]
    rows = l // GRID_W
    n_loc = min(NA_KH, rows) * GRID_W
    cos, sin = _na_tables(l)
    bias = _na_bias_table(rpb, rows)
    hb = lambda off: pl.BlockSpec((1, 1, l, LANES), lambda j, i, off=off: (i, off + j, 0, 0))
    cb = lambda off: pl.BlockSpec((1, 1, lc, LANES), lambda j, i, off=off: (i, off + j, 0, 0))
    const = lambda shape: pl.BlockSpec(shape, lambda j, i: (0,) * len(shape))
    return pl.pallas_call(
        _na_kernel,
        grid=(N_HEADS, b),
        in_specs=[hb(4 * N_HEADS), hb(5 * N_HEADS), hb(6 * N_HEADS), cb(3 * N_HEADS), cb(4 * N_HEADS),
                  const((1, LANES)), const((1, LANES)), const((l, LANES)), const((l, LANES)),
                  pl.BlockSpec((1, NA_KH, GRID_W, n_loc), lambda j, i: (j, 0, 0, 0))],
        out_specs=pl.BlockSpec((1, 1, l, LANES), lambda j, i: (i, j, 0, 0)),
        out_shape=jax.ShapeDtypeStruct((b, N_HEADS, l, LANES), BF16),
        scratch_shapes=[pltpu.VMEM((l, LANES), BF16), pltpu.VMEM((l, LANES), BF16),
                        pltpu.VMEM((lc, LANES), BF16)],
        compiler_params=_cparams("parallel", "arbitrary"),
        name="na_branch",
    )(p_lat, p_lat, p_lat, p_ctx, p_ctx, q_norm_w[_rope_perm()].reshape(1, LANES),
      k_norm_w[_rope_perm()].reshape(1, LANES), cos, sin, bias)


def _merge_kernel(a_ref, b_ref, wa_ref, wb_ref, ga_ref, gb_ref, o_ref, a_s, b_s):
    @pl.when(pl.program_id(1) == 0)
    def _():
        for k in range(N_HEADS):
            a_s[:, k * LANES:(k + 1) * LANES] = a_ref[0, k]
            b_s[:, k * LANES:(k + 1) * LANES] = b_ref[0, k]

    ya = jnp.dot(a_s[...], wa_ref[...], preferred_element_type=F32)
    yb = jnp.dot(b_s[...], wb_ref[...], preferred_element_type=F32)
    for k in range(ga_ref.shape[1]):
        cols = slice(k * LANES, (k + 1) * LANES)
        o_ref[:, cols] = (_sigmoid(ga_ref[0, k].astype(F32)) * ya[:, cols]
                          + _sigmoid(gb_ref[0, k].astype(F32)) * yb[:, cols]).astype(o_ref.dtype)


def _merge(dn_o, na_o, w_a, w_b, p_lat, tm=512, tn=512):
    b, _, l, _ = dn_o.shape
    d = w_a.shape[1]
    tpb = l // tm
    nb = tn // LANES
    head_blk = pl.BlockSpec((1, N_HEADS, tm, LANES), lambda i, j: (i // tpb, 0, i % tpb, 0))
    gate_blk = lambda off: pl.BlockSpec((1, nb, tm, LANES),
                                        lambda i, j, off=off: (i // tpb, off // nb + j, i % tpb, 0))
    w_blk = pl.BlockSpec((N_HEADS * LANES, tn), lambda i, j: (0, j))
    return pl.pallas_call(
        _merge_kernel,
        grid=(b * tpb, d // tn),
        in_specs=[head_blk, head_blk, w_blk, w_blk, gate_blk(7 * N_HEADS), gate_blk(8 * N_HEADS)],
        out_specs=pl.BlockSpec((tm, tn), lambda i, j: (i, j)),
        out_shape=jax.ShapeDtypeStruct((b * l, d), BF16),
        scratch_shapes=[pltpu.VMEM((tm, N_HEADS * LANES), BF16), pltpu.VMEM((tm, N_HEADS * LANES), BF16)],
        compiler_params=_cparams("parallel", "arbitrary"),
        name="merge",
    )(dn_o, na_o, w_a, w_b, p_lat, p_lat)


def _out_kernel(y_ref, w_ref, x_ref, g1_ref, nw_ref, sc_ref, sh_ref, wr_hi_ref, wr_lo_ref, x1_ref, h2_ref, lg_ref):
    x1 = x_ref[...] + g1_ref[0] * jnp.dot(y_ref[...], w_ref[...], preferred_element_type=F32)
    x1_ref[...] = x1
    h2 = (x1 * lax.rsqrt(jnp.mean(x1 * x1, axis=-1, keepdims=True) + EPS) * nw_ref[...]
          * (1.0 + sc_ref[0]) + sh_ref[0])
    h_hi = h2.astype(BF16)
    h2_ref[...] = h_hi
    h_lo = (h2 - h_hi.astype(F32)).astype(BF16)
    lg_ref[...] = (jnp.dot(h_hi, wr_hi_ref[...], preferred_element_type=F32)
                   + (jnp.dot(h_hi, wr_lo_ref[...], preferred_element_type=F32)
                      + jnp.dot(h_lo, wr_hi_ref[...], preferred_element_type=F32)))


def _out_proj(y, w_out, x2d, g1, norm_w, scale, shift, w_router_pad, l, tm=256):
    m, d = x2d.shape
    tpb = l // tm
    wr_hi = w_router_pad.astype(BF16)
    wr_lo = (w_router_pad - wr_hi.astype(F32)).astype(BF16)
    row_blk = lambda: pl.BlockSpec((tm, d), lambda i: (i, 0))
    mod_blk = lambda: pl.BlockSpec((1, 1, d), lambda i: (i // tpb, 0, 0))
    return pl.pallas_call(
        _out_kernel,
        grid=(m // tm,),
        in_specs=[row_blk(), pl.BlockSpec((d, d), lambda i: (0, 0)), row_blk(), mod_blk(),
                  pl.BlockSpec((1, d), lambda i: (0, 0)), mod_blk(), mod_blk(),
                  pl.BlockSpec((d, LANES), lambda i: (0, 0)), pl.BlockSpec((d, LANES), lambda i: (0, 0))],
        out_specs=[row_blk(), row_blk(), pl.BlockSpec((tm, LANES), lambda i: (i, 0))],
        out_shape=[jax.ShapeDtypeStruct((m, d), F32), jax.ShapeDtypeStruct((m, d), BF16),
                   jax.ShapeDtypeStruct((m, LANES), F32)],
        compiler_params=_cparams("parallel"),
        name="out_proj",
    )(y, w_out, x2d, g1, norm_w.reshape(1, d), scale, shift, wr_hi, wr_lo)


def _route_kernel(lg_ref, slot_t_ref, gate_t_ref, slot_ref, aff_s, slot_s):
    l = lg_ref.shape[0]
    n_tiles = l // LANES
    cap = EC_CAPACITY_FACTOR * l // N_EXPERTS
    lane = lax.broadcasted_iota(jnp.int32, (LANES, LANES), 1)
    row = lax.broadcasted_iota(jnp.int32, (LANES, LANES), 0)

    for t in range(n_tiles):
        x = jnp.where(lane < N_EXPERTS, lg_ref[t * LANES:(t + 1) * LANES, :], -jnp.inf)
        e = jnp.exp(x - jnp.max(x, axis=-1, keepdims=True))
        aff = e / jnp.sum(e, axis=-1, keepdims=True)
        aff_s[:, t * LANES:(t + 1) * LANES] = aff.T[:N_EXPERTS]

    aff_t = aff_s[...]
    keys = pltpu.bitcast(aff_t, jnp.int32)
    count_ge = lambda thr: jnp.sum(jnp.where(keys >= thr, 1.0, 0.0), axis=1, keepdims=True)

    def bisect(_, c):
        lo, hi = c
        mid = lo + (hi - lo) // 2
        ok = count_ge(mid) >= cap
        return jnp.where(ok, mid, lo), jnp.where(ok, hi, mid)

    inf_bits = 0x7F800000
    thr, _ = lax.fori_loop(0, 31, bisect, (jnp.zeros((N_EXPERTS, 1), jnp.int32),
                                           jnp.full((N_EXPERTS, 1), inf_bits, jnp.int32)))
    gt, eq = keys > thr, keys == thr
    need = cap - jnp.sum(jnp.where(gt, 1.0, 0.0), axis=1, keepdims=True)
    before = jnp.where(row < lane, 1.0, 0.0).astype(BF16)

    def excl_prefix(flags):
        out, off = [], jnp.zeros((N_EXPERTS, 1), F32)
        ones = jnp.where(flags, 1.0, 0.0)
        for t in range(n_tiles):
            f = ones[:, t * LANES:(t + 1) * LANES]
            out.append(jnp.dot(f.astype(BF16), before, preferred_element_type=F32) + off)
            off = off + jnp.sum(f, axis=1, keepdims=True)
        return jnp.concatenate(out, axis=1)

    sel = gt | (eq & (excl_prefix(eq) < need))
    slot_f = jnp.where(sel, excl_prefix(sel), -1.0)
    slot_t_ref[0] = slot_f.astype(jnp.int32)
    gate_t_ref[0] = aff_t
    slot_s[...] = jnp.full(slot_s.shape, -1.0, F32)
    slot_s[0:N_EXPERTS, :] = slot_f
    for t in range(n_tiles):
        slot_ref[0, t * LANES:(t + 1) * LANES, :] = slot_s[:, t * LANES:(t + 1) * LANES].T.astype(jnp.int32)


def _route(logits, b, l):
    return pl.pallas_call(
        _route_kernel,
        grid=(b,),
        in_specs=[pl.BlockSpec((l, LANES), lambda i: (i, 0))],
        out_specs=[pl.BlockSpec((1, N_EXPERTS, l), lambda i: (i, 0, 0)),
                   pl.BlockSpec((1, N_EXPERTS, l), lambda i: (i, 0, 0)),
                   pl.BlockSpec((1, l, LANES), lambda i: (i, 0, 0))],
        out_shape=[jax.ShapeDtypeStruct((b, N_EXPERTS, l), jnp.int32),
                   jax.ShapeDtypeStruct((b, N_EXPERTS, l), F32),
                   jax.ShapeDtypeStruct((b, l, LANES), jnp.int32)],
        scratch_shapes=[pltpu.VMEM((N_EXPERTS, l), F32), pltpu.VMEM((LANES, l), F32)],
        compiler_params=_cparams("parallel"),
        name="route",
    )(logits)


def _ffn_kernel(h_ref, slot_t_ref, gate_t_ref, w1_ref, w3_ref, w2_ref, o_ref):
    e = pl.program_id(0)
    cap, l = o_ref.shape[2], h_ref.shape[1]
    hit = lax.broadcasted_iota(jnp.int32, (cap, l), 0) == slot_t_ref[0, pl.ds(e, 1), :]
    gate = jnp.sum(jnp.where(hit, gate_t_ref[0, pl.ds(e, 1), :], 0.0), axis=1, keepdims=True)
    x = jnp.dot(jnp.where(hit, 1.0, 0.0).astype(BF16), h_ref[0], preferred_element_type=F32).astype(BF16)
    h1 = jnp.dot(x, w1_ref[0], preferred_element_type=F32)
    h3 = jnp.dot(x, w3_ref[0], preferred_element_type=F32)
    hid = (_silu(h1) * h3).astype(BF16)
    o_ref[0, 0] = (jnp.dot(hid, w2_ref[0], preferred_element_type=F32) * gate).astype(o_ref.dtype)


def _expert_ffn(h2, slot_t, gate_t, w1, w3, w2, cap):
    b, l, d = h2.shape
    e, _, f = w1.shape
    return pl.pallas_call(
        _ffn_kernel,
        grid=(e, b),
        in_specs=[pl.BlockSpec((1, l, d), lambda j, i: (i, 0, 0)),
                  pl.BlockSpec((1, e, l), lambda j, i: (i, 0, 0)),
                  pl.BlockSpec((1, e, l), lambda j, i: (i, 0, 0)),
                  pl.BlockSpec((1, d, f), lambda j, i: (j, 0, 0)),
                  pl.BlockSpec((1, d, f), lambda j, i: (j, 0, 0)),
                  pl.BlockSpec((1, f, d), lambda j, i: (j, 0, 0))],
        out_specs=pl.BlockSpec((1, 1, cap, d), lambda j, i: (i, j, 0, 0)),
        out_shape=jax.ShapeDtypeStruct((b, e, cap, d), BF16),
        compiler_params=_cparams("parallel", "arbitrary"),
        name="expert_ffn",
    )(h2, slot_t, gate_t, w1, w3, w2)


def _combine_kernel(slot_ref, y_ref, x1_ref, g2_ref, o_ref):
    tm = slot_ref.shape[1]
    cap = y_ref.shape[1] // N_EXPERTS
    slot = slot_ref[0]
    j = lax.broadcasted_iota(jnp.int32, (tm, cap), 1)
    onehot = jnp.concatenate([jnp.where(slot[:, e:e + 1] == j, 1.0, 0.0).astype(BF16) for e in range(N_EXPERTS)],
                             axis=1)
    o_ref[0] = x1_ref[0] + g2_ref[0] * jnp.dot(onehot, y_ref[0], preferred_element_type=F32)


def _combine(slot, ye, x1, g2, tm=512, tn=1024):
    b, l, d = x1.shape
    ec = ye.shape[1]
    return pl.pallas_call(
        _combine_kernel,
        grid=(b, d // tn, l // tm),
        in_specs=[pl.BlockSpec((1, tm, LANES), lambda i, n, m: (i, m, 0)),
                  pl.BlockSpec((1, ec, tn), lambda i, n, m: (i, 0, n)),
                  pl.BlockSpec((1, tm, tn), lambda i, n, m: (i, m, n)),
                  pl.BlockSpec((1, 1, tn), lambda i, n, m: (i, 0, n))],
        out_specs=pl.BlockSpec((1, tm, tn), lambda i, n, m: (i, m, n)),
        out_shape=jax.ShapeDtypeStruct((b, l, d), F32),
        compiler_params=_cparams("parallel", "parallel", "arbitrary"),
        name="moe_combine",
    )(slot, ye, x1, g2)


def _layer(x, ctx, mod, mod_c, norm1_w, w_in, conv_w, a_log, dt_bias, dn_norm_w, q_norm_w, k_norm_w, rpb,
           w_a, w_b, w_out, norm2_w, w_router, w1, w3, w2):
    b, l, d = x.shape
    sh1, sc1, g1, sh2, sc2, g2 = [m[:, None, :] for m in jnp.split(mod, 6, axis=-1)]
    sh1c, sc1c = mod_c[None, None, :d], mod_c[None, None, d:2 * d]

    hd = N_HEADS * HEAD_DIM
    offs = np.cumsum([0, hd, hd, hd, hd, 2 * N_HEADS, 2 * N_HEADS, hd, hd, hd, d, d])
    col = lambda k: w_in[:, offs[k]:offs[k + 1]]
    perm = _rope_perm()
    rot = lambda k: col(k).reshape(d, N_HEADS, HEAD_DIM)[:, :, perm].reshape(d, hd)
    w_main = jnp.concatenate([col(0), col(1), col(2), col(3), rot(6), rot(7), col(8), col(9), col(10)],
                             axis=1).astype(BF16)
    w_ctx = jnp.concatenate([col(0), col(1), col(2), rot(7), col(8)], axis=1).astype(BF16)
    w_ba = jnp.pad(jnp.concatenate([col(4), col(5)], axis=1), ((0, 0), (0, LANES - 4 * N_HEADS))).astype(BF16)

    p_lat, ba = _norm_proj(x, norm1_w, sc1, sh1, w_main, w_ba, tm=1024, tn=512)
    p_ctx, cba = _norm_proj(ctx, norm1_w, sc1c, sh1c, w_ctx, w_ba, tm=min(1024, b * ctx.shape[1]), tn=512)

    lanes_pad = lambda v: jnp.pad(v.reshape(1, -1).astype(F32), ((0, 0), (2 * N_HEADS, LANES - 4 * N_HEADS)))
    alog_l, dtb_l = lanes_pad(a_log), lanes_pad(dt_bias)
    bg, cbg = _dn_gates(ba, alog_l, dtb_l), _dn_gates(cba, alog_l, dtb_l)
    taps = jnp.pad(conv_w.astype(F32), ((0, 8 - CONV_K), (0, 0))).reshape(8, 3 * N_HEADS, LANES).transpose(1, 0, 2)
    dn_o = _delta_branch(p_lat, p_ctx, bg, cbg, taps, dn_norm_w.astype(F32))
    na_o = _na_branch(p_lat, p_ctx, q_norm_w.astype(F32), k_norm_w.astype(F32), rpb)

    y = _merge(dn_o, na_o, w_a.astype(BF16), w_b.astype(BF16), p_lat)
    w_router_pad = jnp.pad(w_router.astype(F32), ((0, 0), (0, LANES - N_EXPERTS)))
    x1, h2, logits = _out_proj(y, w_out.astype(BF16), x.reshape(b * l, d), g1, norm2_w, sc2, sh2,
                               w_router_pad, l)

    cap = EC_CAPACITY_FACTOR * l // N_EXPERTS
    slot_t, gate_t, slot = _route(logits, b, l)
    ye = _expert_ffn(h2.reshape(b, l, d), slot_t, gate_t, w1.astype(BF16), w3.astype(BF16), w2.astype(BF16), cap)
    return _combine(slot, ye.reshape(b, N_EXPERTS * cap, d), x1.reshape(b, l, d), g2)


def kernel(x, c, ctx, c_ctx, ada_w, ada_b, norm1_w, w_in, conv_w, dn_a_log, dn_dt_bias, dn_norm_w,
           na_q_norm_w, na_k_norm_w, na_rpb, w_branch_a, w_branch_b, w_out, norm2_w, w_router,
           expert_w1, expert_w3, expert_w2):
    b = x.shape[0]
    depth = ada_w.shape[0]
    cvec = jnp.concatenate([c, c_ctx[None, :], jnp.zeros((16 - b - 1, c.shape[1]), c.dtype)], axis=0)
    for i in range(depth):
        mod_all = _adaln_mod(cvec, ada_w[i], ada_b[i])
        x = _layer(x, ctx, mod_all[:b], mod_all[b], norm1_w[i], w_in[i], conv_w[i], dn_a_log[i],
                   dn_dt_bias[i], dn_norm_w[i], na_q_norm_w[i], na_k_norm_w[i], na_rpb[i],
                   w_branch_a[i], w_branch_b[i], w_out[i], norm2_w[i], w_router[i],
                   expert_w1[i], expert_w3[i], expert_w2[i])
    return x
```

```python
import functools
import math

import numpy as np
import jax
import jax.numpy as jnp
from jax import lax
from jax.experimental import pallas as pl
from jax.experimental.pallas import tpu as pltpu

F32 = jnp.float32
BF16 = jnp.bfloat16

EPS = 1e-6
NEG_INF = -1e30
LANES = 128
GRID_W = 64
N_HEADS = 16
HEAD_DIM = 128
CONV_K = 5
CONV_HALO = 16
DN_CHUNK = 64
DN_GROUP = 8
DN_SOLVE_BASE = 16
NA_KH = 8
NA_KW = 16
NA_ROW_GROUP = 8
ROPE_THETA = 10000.0
N_EXPERTS = 16
EC_CAPACITY_FACTOR = 2
VMEM_LIMIT = 56 * 1024 * 1024


def _cparams(*sem):
    return pltpu.CompilerParams(dimension_semantics=sem, vmem_limit_bytes=VMEM_LIMIT)


def _sigmoid(x):
    return 1.0 / (1.0 + jnp.exp(-x))


def _silu(x):
    return x * _sigmoid(x)


def _mod_kernel(c_ref, w_ref, b_ref, o_ref):
    s = _silu(c_ref[...])
    o_ref[...] = jnp.dot(s, w_ref[...], preferred_element_type=F32,
                         precision=lax.Precision.HIGHEST) + b_ref[...]


def _adaln_mod(cvec, ada_w, ada_b, tn=1024):
    m, d = cvec.shape
    n = ada_w.shape[1]
    return pl.pallas_call(
        _mod_kernel,
        grid=(n // tn,),
        in_specs=[pl.BlockSpec((m, d), lambda j: (0, 0)),
                  pl.BlockSpec((d, tn), lambda j: (0, j)),
                  pl.BlockSpec((1, tn), lambda j: (0, j))],
        out_specs=pl.BlockSpec((m, tn), lambda j: (0, j)),
        out_shape=jax.ShapeDtypeStruct((m, n), F32),
        compiler_params=_cparams("parallel"),
        name="adaln_mod",
    )(cvec, ada_w, ada_b.reshape(1, n))


def _norm_proj_kernel(x_ref, nw_ref, sc_ref, sh_ref, w_ref, wg_ref, o_ref, og_ref, h_ref):
    n_seq, _, rows, _ = o_ref.shape

    @pl.when(pl.program_id(1) == 0)
    def _():
        x = x_ref[...]
        y = x * lax.rsqrt(jnp.mean(x * x, axis=-1, keepdims=True) + EPS) * nw_ref[...]
        h_ref[...] = (y * (1.0 + sc_ref[0]) + sh_ref[0]).astype(BF16)
        og = jnp.dot(h_ref[...], wg_ref[...], preferred_element_type=F32)
        for s in range(n_seq):
            og_ref[s] = og[s * rows:(s + 1) * rows]

    acc =jnp.dot(h_ref[...], w_ref[...], preferred_element_type=F32)
    for s in range(n_seq):
        for k in range(o_ref.shape[1]):
            o_ref[s, k] = acc[s * rows:(s + 1) * rows, k * LANES:(k + 1) * LANES].astype(o_ref.dtype)


def _norm_proj(x, norm_w, scale, shift, w, w_gates, tm, tn):
    b, l, d = x.shape
    n = w.shape[1]
    tpb = max(l // tm, 1)
    spt = max(tm // l, 1)
    per_sample = scale.shape[0] != 1
    assert spt == 1 or not per_sample
    mod_idx = (lambda i, j: (i // tpb, 0, 0)) if per_sample else (lambda i, j: (0, 0, 0))
    return pl.pallas_call(
        _norm_proj_kernel,
        grid=(b * l // tm, n // tn),
        in_specs=[pl.BlockSpec((tm, d), lambda i, j: (i, 0)),
                  pl.BlockSpec((1, d), lambda i, j: (0, 0)),
                  pl.BlockSpec((1, 1, d), mod_idx),
                  pl.BlockSpec((1, 1, d), mod_idx),
                  pl.BlockSpec((d, tn), lambda i, j: (0, j)),
                  pl.BlockSpec((d, LANES), lambda i, j: (0, 0))],
        out_specs=[pl.BlockSpec((spt, tn // LANES, tm // spt, LANES), lambda i, j: (i // tpb, j, i % tpb, 0)),
                   pl.BlockSpec((spt, tm // spt, LANES), lambda i, j: (i // tpb, i % tpb, 0))],
        out_shape=[jax.ShapeDtypeStruct((b, n // LANES, l, LANES), BF16),
                   jax.ShapeDtypeStruct((b, l, LANES), F32)],
        scratch_shapes=[pltpu.VMEM((tm, d), BF16)],
        compiler_params=_cparams("parallel", "arbitrary"),
        name="norm_proj",
    )(x.reshape(b * l, d), norm_w.reshape(1, d), scale, shift, w, w_gates)


def _dn_gates_kernel(x_ref, alog_ref, dtb_ref, o_ref):
    l = x_ref.shape[1]
    lane = lax.broadcasted_iota(jnp.int32, (LANES, LANES), 1)
    row = lax.broadcasted_iota(jnp.int32, (LANES, LANES), 0)
    same_chunk = (row // DN_CHUNK) == (lane // DN_CHUNK)
    prefix_m = jnp.where(same_chunk & (lane <= row), 1.0, 0.0).astype(F32)
    suffix_m = jnp.where(same_chunk & (lane >= row), 1.0, 0.0).astype(F32)
    neg_a = -jnp.exp(alog_ref[...])
    dtb = dtb_ref[...]

    def tile(t, carry):
        rows = pl.ds(pl.multiple_of(t * LANES, LANES), LANES)
        x = x_ref[0, rows, :]
        beta = _sigmoid(x)
        z = x + dtb
        g = neg_a * (jnp.maximum(z, 0.0) + jnp.log1p(jnp.exp(-jnp.abs(z))))
        pre = jnp.dot(prefix_m, g, preferred_element_type=F32, precision=lax.Precision.HIGHEST)
        suf = jnp.dot(suffix_m, g, preferred_element_type=F32, precision=lax.Precision.HIGHEST)
        o_ref[0, rows, :] = jnp.where(lane < 2 * N_HEADS, beta, jnp.where(lane < 3 * N_HEADS, pre, suf))
        return carry

    lax.fori_loop(0, l // LANES, tile, 0)


def _dn_gates(ba, alog_lanes, dtb_lanes):
    b, l, _ = ba.shape
    return pl.pallas_call(
        _dn_gates_kernel,
        grid=(b,),
        in_specs=[pl.BlockSpec((1, l, LANES), lambda i: (i, 0, 0)),
                  pl.BlockSpec((1, LANES), lambda i: (0, 0)),
                  pl.BlockSpec((1, LANES), lambda i: (0, 0))],
        out_specs=pl.BlockSpec((1, l, LANES), lambda i: (i, 0, 0)),
        out_shape=jax.ShapeDtypeStruct((b, l, LANES), F32),
        compiler_params=_cparams("parallel"),
        name="dn_gates",
    )(ba, alog_lanes, dtb_lanes)


def _dn_masks():
    row = lax.broadcasted_iota(jnp.int32, (LANES, LANES), 0)
    col = lax.broadcasted_iota(jnp.int32, (LANES, LANES), 1)
    fwd = row < DN_CHUNK
    same = (row // DN_CHUNK) == (col // DN_CHUNK)
    tril = same & ((fwd & (row >= col)) | (~fwd & (row <= col)))
    strict = tril & (row != col)
    return row, col, tril, strict


def _dn_prepare(h, n_rows, raw_refs, conv_refs, bg_ref, pad_ref, qkv_s, sel_s):
    n_tiles = n_rows // LANES
    zeros_halo = jnp.zeros((CONV_HALO, LANES), F32)
    for which in range(3):
        pad_ref[which, 0:CONV_HALO, :] = zeros_halo
        pad_ref[which, CONV_HALO + n_rows:2 * CONV_HALO + n_rows, :] = zeros_halo
        pad_ref[which, CONV_HALO:CONV_HALO + n_rows, :] = raw_refs[which][0, 0].astype(F32)
    taps = [conv_refs[which][0] for which in range(3)]

    def tile(t, carry):
        rows = pl.ds(pl.multiple_of(t * LANES, LANES), LANES)
        base = t * LANES + CONV_HALO - CONV_K // 2
        ys = []
        for which in range(3):
            acc = pad_ref[which, pl.ds(base, LANES), :] * taps[which][0:1, :]
            for j in range(1, CONV_K):
                acc = acc + pad_ref[which, pl.ds(base + j, LANES), :] * taps[which][j:j + 1, :]
            ys.append(_silu(acc))
        q, k, v = ys
        qkv_s[0, rows, :] = q * (lax.rsqrt(jnp.sum(q * q, axis=-1, keepdims=True) + EPS) * (HEAD_DIM ** -0.5))
        qkv_s[1, rows, :] = k * lax.rsqrt(jnp.sum(k * k, axis=-1, keepdims=True) + EPS)
        qkv_s[2, rows, :] = v
        x = pltpu.roll(bg_ref[0, rows, :], (LANES - h) % LANES, axis=1)
        for s in range(4):
            sel_s[s, rows, :] = jnp.broadcast_to(x[:, s * N_HEADS:s * N_HEADS + 1], x.shape)
        return carry

    lax.fori_loop(0, n_tiles, tile, 0, unroll=2)


def _mm(x, y):
    return jnp.dot(x.astype(BF16), y.astype(BF16), preferred_element_type=F32)


def _unit_tri_solve(a_all, rhs_all, row, col, tick):
    blk = lambda m: (row // m) == (col // m)
    eye = jnp.where(row == col, 1.0, 0.0)
    ad = [jnp.where(blk(DN_SOLVE_BASE), a, 0.0) for a in a_all]
    t = [eye - x for x in ad]
    p = [_mm(x, x) for x in ad]
    tick()
    for _ in range(int(math.log2(DN_SOLVE_BASE)) - 2):
        tp = [_mm(jnp.concatenate([ti, pi], axis=0), pi) for ti, pi in zip(t, p)]
        tick()
        t = [ti + x[:LANES] for ti, x in zip(t, tp)]
        p = [x[LANES:] for x in tp]
    tp = [_mm(ti, pi) for ti, pi in zip(t, p)]
    tick()
    t = [ti + x for ti, x in zip(t, tp)]
    k = 2 * DN_SOLVE_BASE
    while k < DN_CHUNK:
        nk = [_mm(ti, jnp.where(blk(k) & ~blk(k // 2), a, 0.0)) for ti, a in zip(t, a_all)]
        tick()
        tn = [_mm(x, ti) for x, ti in zip(nk, t)]
        tick()
        t = [ti - x for ti, x in zip(t, tn)]
        k *= 2
    ny = [_mm(ti, jnp.concatenate([jnp.where(blk(k // 2), 0.0, a), r], axis=1))
          for ti, a, r in zip(t, a_all, rhs_all)]
    tick()
    ny2 = [_mm(x[:, :LANES], x[:, LANES:]) for x in ny]
    tick()
    return [x[:, LANES:] - z for x, z in zip(ny, ny2)]


def _dn_intra(steps, n, qkv_s, sel_s, lhs_s, c_s, o0_s, eg_s, with_out, slot0=0, tick=lambda: None):
    row, col, tril, strict = _dn_masks()
    is_f = row < DN_CHUNK
    nt = (((1,), (1,)), ((), ()))

    def load(i):
        ri = pl.ds(pl.multiple_of(i * DN_CHUNK, DN_CHUNK), DN_CHUNK)
        rj = pl.ds(pl.multiple_of((n - 1 - i) * DN_CHUNK, DN_CHUNK), DN_CHUNK)
        pair = lambda ref, a, b: jnp.concatenate([ref[a, ri, :], ref[b, rj, :]], axis=0)
        q2, k2, v2 = pair(qkv_s, 0, 0), pair(qkv_s, 1, 1), pair(qkv_s, 2, 2)
        b2, g2 = pair(sel_s, 0, 1), pair(sel_s, 2, 3)
        decay = jnp.exp(jnp.where(tril, g2 - g2.T, -jnp.inf))
        return dict(q2=q2, k2=k2, v2=v2, b2=b2, g2=g2, decay=decay, kb2=k2 * b2, eg=jnp.exp(g2))

    st = [load(i) for i in steps]
    if with_out:
        kk = [lax.dot_general(jnp.concatenate([d["kb2"], d["q2"]], axis=0).astype(BF16), d["k2"].astype(BF16), nt,
                              preferred_element_type=F32) for d in st]
        a_qk = [jnp.where(tril, x[LANES:] * d["decay"], 0.0) for x, d in zip(kk, st)]
    else:
        kk = [lax.dot_general(d["kb2"].astype(BF16), d["k2"].astype(BF16), nt, preferred_element_type=F32)
              for d in st]
    tick()
    a = [jnp.where(strict, x[:LANES] * d["decay"], 0.0) for x, d in zip(kk, st)]
    sol = _unit_tri_solve(a, [jnp.concatenate([d["v2"] * d["b2"], d["kb2"] * d["eg"]], axis=1) for d in st],
                          row, col, tick)
    split = lambda x: [jnp.where(is_f, x, 0.0), jnp.where(is_f, 0.0, x)]
    kwu = []
    for d, x in zip(st, sol):
        g2 = d["g2"]
        gl_f, gl_b = g2[DN_CHUNK - 1:DN_CHUNK, :], g2[DN_CHUNK:DN_CHUNK + 1, :]
        gl = jnp.concatenate([jnp.broadcast_to(gl_f, (DN_CHUNK, LANES)),
                              jnp.broadcast_to(gl_b, (DN_CHUNK, LANES))], axis=0)
        kd2 = d["k2"] * jnp.exp(gl - g2)
        d["egl"] = jnp.exp(jnp.concatenate([gl_f, gl_b, jnp.zeros((6, LANES), F32)], axis=0))
        kwu.append(_mm(kd2.T, jnp.concatenate(split(x[:, LANES:]) + split(x[:, :LANES]), axis=1)))
    if with_out:
        awu = [_mm(x, y) for x, y in zip(a_qk, sol)]
    tick()
    for k, i in enumerate(steps):
        i = slot0 + i
        c_s[i, 0] = kwu[k][:, 2 * LANES:3 * LANES]
        c_s[i, 1] = kwu[k][:, 3 * LANES:]
        eg_s[i] = st[k]["egl"]
        lhs_s[i, 0, 0:LANES, :] = (-kwu[k][:, :LANES]).astype(BF16)
        lhs_s[i, 1, 0:LANES, :] = (-kwu[k][:, LANES:2 * LANES]).astype(BF16)
        if with_out:
            q_eff = st[k]["q2"] * st[k]["eg"] - awu[k][:, LANES:]
            o0_s[i] = awu[k][:, :LANES]
            lhs_s[i, 0, LANES:LANES + DN_CHUNK, :] = q_eff[:DN_CHUNK].astype(BF16)
            lhs_s[i, 1, LANES:LANES + DN_CHUNK, :] = q_eff[DN_CHUNK:].astype(BF16)


def _dn_recur(i, n, s_f, s_b, lhs_s, c_s, o0_s, eg_s, o_s, with_out, slot0=0):
    m = LANES + DN_CHUNK if with_out else LANES
    k = slot0 + i
    r_f = jnp.dot(lhs_s[k, 0, 0:m, :], s_f.astype(BF16), preferred_element_type=F32)
    r_b = jnp.dot(lhs_s[k, 1, 0:m, :], s_b.astype(BF16), preferred_element_type=F32)
    if with_out:
        o0 = o0_s[i]
        o_s[0, pl.ds(pl.multiple_of(i * DN_CHUNK, DN_CHUNK), DN_CHUNK), :] = r_f[LANES:] + o0[:DN_CHUNK]
        o_s[1, pl.ds(pl.multiple_of((n - 1 - i) * DN_CHUNK, DN_CHUNK), DN_CHUNK), :] = r_b[LANES:] + o0[DN_CHUNK:]
    eg = eg_s[k]
    s_f = s_f * jnp.broadcast_to(eg[0:1, :], (LANES, LANES)) + r_f[:LANES] + c_s[k, 0]
    s_b = s_b * jnp.broadcast_to(eg[1:2, :], (LANES, LANES)) + r_b[:LANES] + c_s[k, 1]
    return s_f, s_b


def _delta_kernel(q_ref, k_ref, v_ref, z_ref, cq_ref, ck_ref, cv_ref, bg_ref, cbg_ref,
                  wq_ref, wk_ref, wv_ref, nw_ref, o_ref,
                  pad_s, qkv_s, sel_s, cqkv_s, csel_s, lhs_s, c_s, o0_s, eg_s, o_s):
    h = pl.program_id(1)
    l = q_ref.shape[2]
    lc = cq_ref.shape[2]
    n, nc = l // DN_CHUNK, lc // DN_CHUNK
    conv_refs = (wq_ref, wk_ref, wv_ref)
    stage = (lhs_s, c_s, o0_s, eg_s)

    _dn_prepare(h, lc, (cq_ref, ck_ref, cv_ref), conv_refs, cbg_ref, pad_s, cqkv_s, csel_s)
    _dn_prepare(h, l, (q_ref, k_ref, v_ref), conv_refs, bg_ref, pad_s, qkv_s, sel_s)

    group = math.gcd(DN_GROUP, n)
    n_groups = n // group
    gc = math.gcd(DN_GROUP, nc)
    for g in range(nc // gc):
        _dn_intra([g * gc + k for k in range(gc)], nc, cqkv_s, csel_s, *stage, with_out=False, slot0=n)
    state = [jnp.zeros((LANES, LANES), F32)] * 2

    def ctx_step(i):
        state[:] = _dn_recur(i, nc, state[0], state[1], *stage, o_s, with_out=False, slot0=n)

    def lat_step(i):
        state[:] = _dn_recur(i, n, state[0], state[1], *stage, o_s, with_out=True)

    def lat_group(g, scan_steps):
        pending = list(scan_steps)
        tick = lambda: pending.pop(0)() if pending else None
        _dn_intra([g * group + k for k in range(group)], n, qkv_s, sel_s, *stage, with_out=True, tick=tick)
        while pending:
            tick()

    lat_group(0, [functools.partial(ctx_step, i) for i in range(nc)])

    def pipelined(g, s):
        state[:] = s
        lat_group(g, [functools.partial(lat_step, (g - 1) * group + k) for k in range(group)])
        return tuple(state)

    state[:] = lax.fori_loop(1, n_groups, pipelined, tuple(state))

    def tail(i, s):
        return _dn_recur(i, n, s[0], s[1], *stage, o_s, with_out=True)

    lax.fori_loop((n_groups - 1) * group, n, tail, tuple(state))

    nw = nw_ref[...]

    def out_tile(t, c):
        rows = pl.ds(pl.multiple_of(t * LANES, LANES), LANES)
        o = o_s[0, rows, :] + o_s[1, rows, :]
        y = o * lax.rsqrt(jnp.mean(o * o, axis=-1, keepdims=True) + EPS) * nw
        o_ref[0, 0, rows, :] = (y * _silu(z_ref[0, 0, rows, :].astype(F32))).astype(o_ref.dtype)
        return c

    lax.fori_loop(0, l // LANES, out_tile, 0, unroll=2)


def _delta_branch(p_lat, p_ctx, bg, cbg, conv_taps, norm_w):
    b, _, l, _ = p_lat.shape
    lc = p_ctx.shape[2]
    n, nc = l // DN_CHUNK, lc // DN_CHUNK
    hb =lambda off: pl.BlockSpec((1, 1, l, LANES), lambda i, j, off=off: (i, off + j, 0, 0))
    cb = lambda off: pl.BlockSpec((1, 1, lc, LANES), lambda i, j, off=off: (i, off + j, 0, 0))
    tb = lambda off: pl.BlockSpec((1, 8, LANES), lambda i, j, off=off: (off + j, 0, 0))
    return pl.pallas_call(
        _delta_kernel,
        grid=(b, N_HEADS),
        in_specs=[hb(0), hb(N_HEADS), hb(2 * N_HEADS), hb(3 * N_HEADS),
                  cb(0), cb(N_HEADS), cb(2 * N_HEADS),
                  pl.BlockSpec((1, l, LANES), lambda i, j: (i, 0, 0)),
                  pl.BlockSpec((1, lc, LANES), lambda i, j: (i, 0, 0)),
                  tb(0), tb(N_HEADS), tb(2 * N_HEADS),
                  pl.BlockSpec((1, LANES), lambda i, j: (0, 0))],
        out_specs=pl.BlockSpec((1, 1, l, LANES), lambda i, j: (i, j, 0, 0)),
        out_shape=jax.ShapeDtypeStruct((b, N_HEADS, l, LANES), BF16),
        scratch_shapes=[pltpu.VMEM((3, l + 2 * CONV_HALO, LANES), F32),
                        pltpu.VMEM((3, l, LANES), F32),
                        pltpu.VMEM((4, l, LANES), F32),
                        pltpu.VMEM((3, lc, LANES), F32),
                        pltpu.VMEM((4, lc, LANES), F32),
                        pltpu.VMEM((n + nc, 2, LANES + DN_CHUNK, LANES), BF16),
                        pltpu.VMEM((n + nc, 2, LANES, LANES), F32),
                        pltpu.VMEM((n, LANES, LANES), F32),
                        pltpu.VMEM((n + nc, 8, LANES), F32),
                        pltpu.VMEM((2, l, LANES), F32)],
        compiler_params=_cparams("parallel", "arbitrary"),
        name="delta_branch",
    )(p_lat, p_lat, p_lat, p_lat, p_ctx, p_ctx, p_ctx, bg, cbg,
      conv_taps, conv_taps, conv_taps, norm_w.reshape(1, LANES))


def _rope_perm():
    quarter = HEAD_DIM // 4
    return np.concatenate([np.arange(quarter), 2 * quarter + np.arange(quarter),
                           quarter + np.arange(quarter), 3 * quarter + np.arange(quarter)])


def _rope(x, cos, sin):
    return x * cos + pltpu.roll(x, HEAD_DIM // 2, axis=1) * sin


def _head_rms(x, w):
    return x * lax.rsqrt(jnp.mean(x * x, axis=-1, keepdims=True) + EPS) * w


def _na_kernel(q_ref, k_ref, v_ref, ck_ref, cv_ref, qnw_ref, knw_ref, cos_ref, sin_ref, bias_ref, o_ref,
               q_s, k_s, ck_s):
    l = q_ref.shape[2]
    rows = l // GRID_W
    kh = min(NA_KH, rows)
    n_loc = kh * GRID_W
    qnw, knw = qnw_ref[...], knw_ref[...]

    def prep(t, c):
        r = pl.ds(pl.multiple_of(t * LANES, LANES), LANES)
        cos, sin = cos_ref[r, :], sin_ref[r, :]
        q = _rope(_head_rms(q_ref[0, 0, r, :].astype(F32), qnw), cos, sin) * (HEAD_DIM ** -0.5)
        q_s[r, :] = q.astype(BF16)
        k_s[r, :] = _rope(_head_rms(k_ref[0, 0, r, :].astype(F32), knw), cos, sin).astype(BF16)
        return c

    lax.fori_loop(0, l // LANES, prep, 0, unroll=2)
    ck_s[...] = _head_rms(ck_ref[0, 0].astype(F32), knw).astype(BF16)
    nt = (((1,), (1,)), ((), ()))

    group = math.gcd(NA_ROW_GROUP, rows)

    def row_group(g, c):
        rr = [g * group + k for k in range(group)]
        rs = [jnp.clip(r - kh // 2, 0, rows - kh) for r in rr]
        qr = [q_s[pl.ds(pl.multiple_of(r * GRID_W, GRID_W), GRID_W), :] for r in rr]
        kloc = [pl.ds(pl.multiple_of(x * GRID_W, GRID_W), n_loc) for x in rs]
        s_loc = [lax.dot_general(q, k_s[kl, :], nt, preferred_element_type=F32) for q, kl in zip(qr, kloc)]
        s_ctx = [lax.dot_general(q, ck_s[...], nt, preferred_element_type=F32) for q in qr]
        s_loc = [s + bias_ref[0, r - x] for s, r, x in zip(s_loc, rr, rs)]
        m = [jnp.maximum(jnp.max(sl, axis=-1, keepdims=True), jnp.max(sc, axis=-1, keepdims=True))
             for sl, sc in zip(s_loc, s_ctx)]
        p_loc = [jnp.exp(s - mi) for s, mi in zip(s_loc, m)]
        p_ctx = [jnp.exp(s - mi) for s, mi in zip(s_ctx, m)]
        denom = [jnp.sum(pl_, axis=-1, keepdims=True) + jnp.sum(pc, axis=-1, keepdims=True)
                 for pl_, pc in zip(p_loc, p_ctx)]
        o_loc = [jnp.dot(p.astype(BF16), v_ref[0, 0, kl, :], preferred_element_type=F32)
                 for p, kl in zip(p_loc, kloc)]
        o_ctx = [jnp.dot(p.astype(BF16), cv_ref[0, 0], preferred_element_type=F32) for p in p_ctx]
        for r, ol, oc, d in zip(rr, o_loc, o_ctx, denom):
            o_ref[0, 0, pl.ds(pl.multiple_of(r * GRID_W, GRID_W), GRID_W), :] = ((ol + oc) / d).astype(o_ref.dtype)
        return c

    lax.fori_loop(0, rows // group, row_group, 0)


def _na_tables(l):
    pos = jnp.arange(l)
    row = (pos // GRID_W).astype(F32)
    col = (pos % GRID_W).astype(F32)
    half = HEAD_DIM // 2
    inv_freq = ROPE_THETA ** (-jnp.arange(0, half, 2, dtype=F32) / half)
    ang_r = row[:, None] * inv_freq[None, :]
    ang_c = col[:, None] * inv_freq[None, :]
    cos = jnp.concatenate([jnp.cos(ang_r), jnp.cos(ang_r), jnp.cos(ang_c), jnp.cos(ang_c)], axis=-1)
    sin = jnp.concatenate([-jnp.sin(ang_r), jnp.sin(ang_r), -jnp.sin(ang_c), jnp.sin(ang_c)], axis=-1)
    perm = _rope_perm()
    return cos[:, perm], sin[:, perm]


def _na_bias_table(rpb, rows):
    kh = min(NA_KH, rows)
    t = np.arange(NA_KH)[:, None]
    i = np.arange(kh)[None, :]
    dr = np.clip(i - t + NA_KH - 1, 0, 2 * NA_KH - 2)
    q = np.arange(GRID_W)[:, None]
    kc = np.arange(GRID_W)[None, :]
    qstart = np.clip(q - NA_KW // 2, 0, GRID_W - NA_KW)
    in_win = (kc >= qstart) & (kc < qstart + NA_KW)
    dc = np.clip(kc - q + NA_KW - 1, 0, 2 * NA_KW - 2)
    pick_r = (dr[:, :, None] == np.arange(2 * NA_KH - 1)).astype(np.float32)
    pick_c = (np.arange(2 * NA_KW - 1)[:, None, None] == dc[None]).astype(np.float32)
    tab = jnp.einsum("tir,hrc,cqk->htqik", pick_r, rpb.astype(F32), pick_c, precision=lax.Precision.HIGHEST)
    tab = jnp.where(jnp.asarray(in_win)[None, None, :, None, :], tab, NEG_INF)
    return tab.reshape(rpb.shape[0], NA_KH, GRID_W, kh * GRID_W)


def _na_branch(p_lat, p_ctx, q_norm_w, k_norm_w, rpb):
    b, _, l, _ = p_lat.shape
    lc = p_ctx.shape[2]
    rows = l // GRID_W
    n_loc = min(NA_KH, rows) * GRID_W
    cos, sin = _na_tables(l)
    bias = _na_bias_table(rpb, rows)
    hb = lambda off: pl.BlockSpec((1, 1, l, LANES), lambda j, i, off=off: (i, off + j, 0, 0))
    cb = lambda off: pl.BlockSpec((1, 1, lc, LANES), lambda j, i, off=off: (i, off + j, 0, 0))
    const = lambda shape: pl.BlockSpec(shape, lambda j, i: (0,) * len(shape))
    return pl.pallas_call(
        _na_kernel,
        grid=(N_HEADS, b),
        in_specs=[hb(4 * N_HEADS), hb(5 * N_HEADS), hb(6 * N_HEADS), cb(3 * N_HEADS), cb(4 * N_HEADS),
                  const((1, LANES)), const((1, LANES)), const((l, LANES)), const((l, LANES)),
                  pl.BlockSpec((1, NA_KH, GRID_W, n_loc), lambda j, i: (j, 0, 0, 0))],
        out_specs=pl.BlockSpec((1, 1, l, LANES), lambda j, i: (i, j, 0, 0)),
        out_shape=jax.ShapeDtypeStruct((b, N_HEADS, l, LANES), BF16),
        scratch_shapes=[pltpu.VMEM((l, LANES), BF16), pltpu.VMEM((l, LANES), BF16),
                        pltpu.VMEM((lc, LANES), BF16)],
        compiler_params=_cparams("parallel", "arbitrary"),
        name="na_branch",
    )(p_lat, p_lat, p_lat, p_ctx, p_ctx, q_norm_w[_rope_perm()].reshape(1, LANES),
      k_norm_w[_rope_perm()].reshape(1, LANES), cos, sin, bias)


def _merge_kernel(a_ref, b_ref, wa_ref, wb_ref, ga_ref, gb_ref, o_ref, a_s, b_s):
    @pl.when(pl.program_id(1) == 0)
    def _():
        for k in range(N_HEADS):
            a_s[:, k * LANES:(k + 1) * LANES] = a_ref[0, k]
            b_s[:, k * LANES:(k + 1) * LANES] = b_ref[0, k]

    ya = jnp.dot(a_s[...], wa_ref[...], preferred_element_type=F32)
    yb = jnp.dot(b_s[...], wb_ref[...], preferred_element_type=F32)
    for k in range(ga_ref.shape[1]):
        cols = slice(k * LANES, (k + 1) * LANES)
        o_ref[:, cols] = (_sigmoid(ga_ref[0, k].astype(F32)) * ya[:, cols]
                          + _sigmoid(gb_ref[0, k].astype(F32)) * yb[:, cols]).astype(o_ref.dtype)


def _merge(dn_o, na_o, w_a, w_b, p_lat, tm=1024, tn=512):
    b, _, l, _ = dn_o.shape
    d = w_a.shape[1]
    tpb = l // tm
    nb = tn // LANES
    head_blk = pl.BlockSpec((1, N_HEADS, tm, LANES), lambda i, j: (i // tpb, 0, i % tpb, 0))
    gate_blk = lambda off: pl.BlockSpec((1, nb, tm, LANES),
                                        lambda i, j, off=off: (i // tpb, off // nb + j, i % tpb, 0))
    w_blk = pl.BlockSpec((N_HEADS * LANES, tn), lambda i, j: (0, j))
    return pl.pallas_call(
        _merge_kernel,
        grid=(b * tpb, d // tn),
        in_specs=[head_blk, head_blk, w_blk, w_blk, gate_blk(7 * N_HEADS), gate_blk(8 * N_HEADS)],
        out_specs=pl.BlockSpec((tm, tn), lambda i, j: (i, j)),
        out_shape=jax.ShapeDtypeStruct((b * l, d), BF16),
        scratch_shapes=[pltpu.VMEM((tm, N_HEADS * LANES), BF16), pltpu.VMEM((tm, N_HEADS * LANES), BF16)],
        compiler_params=_cparams("parallel", "arbitrary"),
        name="merge",
    )(dn_o, na_o, w_a, w_b, p_lat, p_lat)


def _out_kernel(y_ref, w_ref, x_ref, g1_ref, nw_ref, sc_ref, sh_ref, wr_hi_ref, wr_lo_ref, x1_ref, h2_ref, lg_ref):
    x1 = x_ref[...] + g1_ref[0] * jnp.dot(y_ref[...], w_ref[...], preferred_element_type=F32)
    x1_ref[...] = x1
    h2 = (x1 * lax.rsqrt(jnp.mean(x1 * x1, axis=-1, keepdims=True) + EPS) * nw_ref[...]
          * (1.0 + sc_ref[0]) + sh_ref[0])
    h_hi = h2.astype(BF16)
    h2_ref[...] = h_hi
    h_lo = (h2 - h_hi.astype(F32)).astype(BF16)
    lg_ref[...] = (jnp.dot(h_hi, wr_hi_ref[...], preferred_element_type=F32)
                   + (jnp.dot(h_hi, wr_lo_ref[...], preferred_element_type=F32)
                      + jnp.dot(h_lo, wr_hi_ref[...], preferred_element_type=F32)))


def _out_proj(y, w_out, x2d, g1, norm_w, scale, shift, w_router_pad, l, tm=256):
    m, d = x2d.shape
    tpb = l // tm
    wr_hi = w_router_pad.astype(BF16)
    wr_lo = (w_router_pad - wr_hi.astype(F32)).astype(BF16)
    row_blk = lambda: pl.BlockSpec((tm, d), lambda i: (i, 0))
    mod_blk = lambda: pl.BlockSpec((1, 1, d), lambda i: (i // tpb, 0, 0))
    return pl.pallas_call(
        _out_kernel,
        grid=(m // tm,),
        in_specs=[row_blk(), pl.BlockSpec((d, d), lambda i: (0, 0)), row_blk(), mod_blk(),
                  pl.BlockSpec((1, d), lambda i: (0, 0)), mod_blk(), mod_blk(),
                  pl.BlockSpec((d, LANES), lambda i: (0, 0)), pl.BlockSpec((d, LANES), lambda i: (0, 0))],
        out_specs=[row_blk(), row_blk(), pl.BlockSpec((tm, LANES), lambda i: (i, 0))],
        out_shape=[jax.ShapeDtypeStruct((m, d), F32), jax.ShapeDtypeStruct((m, d), BF16),
                   jax.ShapeDtypeStruct((m, LANES), F32)],
        compiler_params=_cparams("parallel"),
        name="out_proj",
    )(y, w_out, x2d, g1, norm_w.reshape(1, d), scale, shift, wr_hi, wr_lo)


def _route_kernel(lg_ref, slot_t_ref, gate_t_ref, slot_ref, aff_s, slot_s):
    l = lg_ref.shape[0]
    n_tiles = l // LANES
    cap = EC_CAPACITY_FACTOR * l // N_EXPERTS
    lane = lax.broadcasted_iota(jnp.int32, (LANES, LANES), 1)
    row = lax.broadcasted_iota(jnp.int32, (LANES, LANES), 0)

    for t in range(n_tiles):
        x = jnp.where(lane < N_EXPERTS, lg_ref[t * LANES:(t + 1) * LANES, :], -jnp.inf)
        e = jnp.exp(x - jnp.max(x, axis=-1, keepdims=True))
        aff = e / jnp.sum(e, axis=-1, keepdims=True)
        aff_s[:, t * LANES:(t + 1) * LANES] = aff.T[:N_EXPERTS]

    aff_t = aff_s[...]
    keys = pltpu.bitcast(aff_t, jnp.int32)
    count_ge = lambda thr: jnp.sum(jnp.where(keys >= thr, 1.0, 0.0), axis=1, keepdims=True)

    def bisect(_, c):
        lo, hi = c
        mid = lo + (hi - lo) // 2
        ok = count_ge(mid) >= cap
        return jnp.where(ok, mid, lo), jnp.where(ok, hi, mid)

    inf_bits = 0x7F800000
    thr, _ = lax.fori_loop(0, 31, bisect, (jnp.zeros((N_EXPERTS, 1), jnp.int32),
                                           jnp.full((N_EXPERTS, 1), inf_bits, jnp.int32)))
    gt, eq = keys > thr, keys == thr
    need = cap - jnp.sum(jnp.where(gt, 1.0, 0.0), axis=1, keepdims=True)
    before = jnp.where(row < lane, 1.0, 0.0).astype(BF16)

    def excl_prefix(flags):
        out, off = [], jnp.zeros((N_EXPERTS, 1), F32)
        ones = jnp.where(flags, 1.0, 0.0)
        for t in range(n_tiles):
            f = ones[:, t * LANES:(t + 1) * LANES]
            out.append(jnp.dot(f.astype(BF16), before, preferred_element_type=F32) + off)
            off = off + jnp.sum(f, axis=1, keepdims=True)
        return jnp.concatenate(out, axis=1)

    sel = gt | (eq & (excl_prefix(eq) < need))
    slot_f = jnp.where(sel, excl_prefix(sel), -1.0)
    slot_t_ref[0] = slot_f.astype(jnp.int32)
    gate_t_ref[0] = aff_t
    slot_s[...] = jnp.full(slot_s.shape, -1.0, F32)
    slot_s[0:N_EXPERTS, :] = slot_f
    for t in range(n_tiles):
        slot_ref[0, t * LANES:(t + 1) * LANES, :] = slot_s[:, t * LANES:(t + 1) * LANES].T.astype(jnp.int32)


def _route(logits, b, l):
    return pl.pallas_call(
        _route_kernel,
        grid=(b,),
        in_specs=[pl.BlockSpec((l, LANES), lambda i: (i, 0))],
        out_specs=[pl.BlockSpec((1, N_EXPERTS, l), lambda i: (i, 0, 0)),
                   pl.BlockSpec((1, N_EXPERTS, l), lambda i: (i, 0, 0)),
                   pl.BlockSpec((1, l, LANES), lambda i: (i, 0, 0))],
        out_shape=[jax.ShapeDtypeStruct((b, N_EXPERTS, l), jnp.int32),
                   jax.ShapeDtypeStruct((b, N_EXPERTS, l), F32),
                   jax.ShapeDtypeStruct((b, l, LANES), jnp.int32)],
        scratch_shapes=[pltpu.VMEM((N_EXPERTS, l), F32), pltpu.VMEM((LANES, l), F32)],
        compiler_params=_cparams("parallel"),
        name="route",
    )(logits)


def _ffn_kernel(h_ref, slot_t_ref, gate_t_ref, w1_ref, w3_ref, w2_ref, o_ref):
    e = pl.program_id(0)
    cap, l = o_ref.shape[2], h_ref.shape[1]
    hit = lax.broadcasted_iota(jnp.int32, (cap, l), 0) == slot_t_ref[0, pl.ds(e, 1), :]
    gate = jnp.sum(jnp.where(hit, gate_t_ref[0, pl.ds(e, 1), :], 0.0), axis=1, keepdims=True)
    x = jnp.dot(jnp.where(hit, 1.0, 0.0).astype(BF16), h_ref[0], preferred_element_type=F32).astype(BF16)
    h1 = jnp.dot(x, w1_ref[0], preferred_element_type=F32)
    h3 = jnp.dot(x, w3_ref[0], preferred_element_type=F32)
    hid = (_silu(h1) * h3).astype(BF16)
    o_ref[0, 0] = (jnp.dot(hid, w2_ref[0], preferred_element_type=F32) * gate).astype(o_ref.dtype)


def _expert_ffn(h2, slot_t, gate_t, w1, w3, w2, cap):
    b, l, d = h2.shape
    e, _, f = w1.shape
    return pl.pallas_call(
        _ffn_kernel,
        grid=(e, b),
        in_specs=[pl.BlockSpec((1, l, d), lambda j, i: (i, 0, 0)),
                  pl.BlockSpec((1, e, l), lambda j, i: (i, 0, 0)),
                  pl.BlockSpec((1, e, l), lambda j, i: (i, 0, 0)),
                  pl.BlockSpec((1, d, f), lambda j, i: (j, 0, 0)),
                  pl.BlockSpec((1, d, f), lambda j, i: (j, 0, 0)),
                  pl.BlockSpec((1, f, d), lambda j, i: (j, 0, 0))],
        out_specs=pl.BlockSpec((1, 1, cap, d), lambda j, i: (i, j, 0, 0)),
        out_shape=jax.ShapeDtypeStruct((b, e, cap, d), BF16),
        compiler_params=_cparams("parallel", "arbitrary"),
        name="expert_ffn",
    )(h2, slot_t, gate_t, w1, w3, w2)


def _combine_kernel(slot_ref, y_ref, x1_ref, g2_ref, o_ref):
    tm = slot_ref.shape[1]
    cap = y_ref.shape[1] // N_EXPERTS
    slot = slot_ref[0]
    j = lax.broadcasted_iota(jnp.int32, (tm, cap), 1)
    onehot = jnp.concatenate([jnp.where(slot[:, e:e + 1] == j, 1.0, 0.0).astype(BF16) for e in range(N_EXPERTS)],
                             axis=1)
    o_ref[0] = x1_ref[0] + g2_ref[0] * jnp.dot(onehot, y_ref[0], preferred_element_type=F32)


def _combine(slot, ye, x1, g2, tm=512, tn=1024):
    b, l, d = x1.shape
    ec = ye.shape[1]
    return pl.pallas_call(
        _combine_kernel,
        grid=(b, d // tn, l // tm),
        in_specs=[pl.BlockSpec((1, tm, LANES), lambda i, n, m: (i, m, 0)),
                  pl.BlockSpec((1, ec, tn), lambda i, n, m: (i, 0, n)),
                  pl.BlockSpec((1, tm, tn), lambda i, n, m: (i, m, n)),
                  pl.BlockSpec((1, 1, tn), lambda i, n, m: (i, 0, n))],
        out_specs=pl.BlockSpec((1, tm, tn), lambda i, n, m: (i, m, n)),
        out_shape=jax.ShapeDtypeStruct((b, l, d), F32),
        compiler_params=_cparams("parallel", "parallel", "arbitrary"),
        name="moe_combine",
    )(slot, ye, x1, g2)


def _layer(x, ctx, mod, mod_c, norm1_w, w_in, conv_w, a_log, dt_bias, dn_norm_w, q_norm_w, k_norm_w, rpb,
           w_a, w_b, w_out, norm2_w, w_router, w1, w3, w2):
    b, l, d = x.shape
    sh1, sc1, g1, sh2, sc2, g2 = [m[:, None, :] for m in jnp.split(mod, 6, axis=-1)]
    sh1c, sc1c = mod_c[None, None, :d], mod_c[None, None, d:2 * d]

    hd = N_HEADS * HEAD_DIM
    offs = np.cumsum([0, hd, hd, hd, hd, 2 * N_HEADS, 2 * N_HEADS, hd, hd, hd, d, d])
    col = lambda k: w_in[:, offs[k]:offs[k + 1]]
    perm = _rope_perm()
    rot = lambda k: col(k).reshape(d, N_HEADS, HEAD_DIM)[:, :, perm].reshape(d, hd)
    w_main = jnp.concatenate([col(0), col(1), col(2), col(3), rot(6), rot(7), col(8), col(9), col(10)],
                             axis=1).astype(BF16)
    w_ctx = jnp.concatenate([col(0), col(1), col(2), rot(7), col(8)], axis=1).astype(BF16)
    w_ba = jnp.pad(jnp.concatenate([col(4), col(5)], axis=1), ((0, 0), (0, LANES - 4 * N_HEADS))).astype(BF16)

    p_lat, ba = _norm_proj(x, norm1_w, sc1, sh1, w_main, w_ba, tm=1024, tn=1024)
    p_ctx, cba = _norm_proj(ctx, norm1_w, sc1c, sh1c, w_ctx, w_ba, tm=min(1024, b * ctx.shape[1]), tn=1024)

    lanes_pad = lambda v: jnp.pad(v.reshape(1, -1).astype(F32), ((0, 0), (2 * N_HEADS, LANES - 4 * N_HEADS)))
    alog_l, dtb_l = lanes_pad(a_log), lanes_pad(dt_bias)
    bg, cbg = _dn_gates(ba, alog_l, dtb_l), _dn_gates(cba, alog_l, dtb_l)
    taps = jnp.pad(conv_w.astype(F32), ((0, 8 - CONV_K), (0, 0))).reshape(8, 3 * N_HEADS, LANES).transpose(1, 0, 2)
    dn_o = _delta_branch(p_lat, p_ctx, bg, cbg, taps, dn_norm_w.astype(F32))
    na_o = _na_branch(p_lat, p_ctx, q_norm_w.astype(F32), k_norm_w.astype(F32), rpb)

    y = _merge(dn_o, na_o, w_a.astype(BF16), w_b.astype(BF16), p_lat)
    w_router_pad = jnp.pad(w_router.astype(F32), ((0, 0), (0, LANES - N_EXPERTS)))
    x1, h2, logits = _out_proj(y, w_out.astype(BF16), x.reshape(b * l, d), g1, norm2_w, sc2, sh2,
                               w_router_pad, l)

    cap = EC_CAPACITY_FACTOR * l // N_EXPERTS
    slot_t, gate_t, slot = _route(logits, b, l)
    ye = _expert_ffn(h2.reshape(b, l, d), slot_t, gate_t, w1.astype(BF16), w3.astype(BF16), w2.astype(BF16), cap)
    return _combine(slot, ye.reshape(b, N_EXPERTS * cap, d), x1.reshape(b, l, d), g2)


def kernel(x, c, ctx, c_ctx, ada_w, ada_b, norm1_w, w_in, conv_w, dn_a_log, dn_dt_bias, dn_norm_w,
           na_q_norm_w, na_k_norm_w, na_rpb, w_branch_a, w_branch_b, w_out, norm2_w, w_router,
           expert_w1, expert_w3, expert_w2):
    b = x.shape[0]
    depth = ada_w.shape[0]
    cvec = jnp.concatenate([c, c_ctx[None, :], jnp.zeros((16 - b - 1, c.shape[1]), c.dtype)], axis=0)
    for i in range(depth):
        mod_all = _adaln_mod(cvec, ada_w[i], ada_b[i])
        x = _layer(x, ctx, mod_all[:b], mod_all[b], norm1_w[i], w_in[i], conv_w[i], dn_a_log[i],
                   dn_dt_bias[i], dn_norm_w[i], na_q_norm_w[i], na_k_norm_w[i], na_rpb[i],
                   w_branch_a[i], w_branch_b[i], w_out[i], norm2_w[i], w_router[i],
                   expert_w1[i], expert_w3[i], expert_w2[i])
    return x
```

```python
import functools
import math

import numpy as np
import jax
import jax.numpy as jnp
from jax import lax
from jax.experimental import pallas as pl
from jax.experimental.pallas import tpu as pltpu

F32 = jnp.float32
BF16 = jnp.bfloat16

EPS = 1e-6
NEG_INF = -1e30
LANES = 128
GRID_W = 64
N_HEADS = 16
HEAD_DIM = 128
CONV_K = 5
CONV_HALO = 16
DN_CHUNK = 64
DN_GROUP = 8
DN_SOLVE_BASE = 16
NA_KH = 8
NA_KW = 16
NA_ROW_GROUP = 8
ROPE_THETA = 10000.0
N_EXPERTS = 16
EC_CAPACITY_FACTOR = 2
VMEM_LIMIT = 56 * 1024 * 1024


def _cparams(*sem):
    return pltpu.CompilerParams(dimension_semantics=sem, vmem_limit_bytes=VMEM_LIMIT)


def _sigmoid(x):
    return 1.0 / (1.0 + jnp.exp(-x))


def _silu(x):
    return x * _sigmoid(x)


def _mod_kernel(c_ref, w_ref, b_ref, o_ref):
    s = _silu(c_ref[...])
    o_ref[...] = jnp.dot(s, w_ref[...], preferred_element_type=F32,
                         precision=lax.Precision.HIGHEST) + b_ref[...]


def _adaln_mod(cvec, ada_w, ada_b, tn=1024):
    m, d = cvec.shape
    n = ada_w.shape[1]
    return pl.pallas_call(
        _mod_kernel,
        grid=(n // tn,),
        in_specs=[pl.BlockSpec((m, d), lambda j: (0, 0)),
                  pl.BlockSpec((d, tn), lambda j: (0, j)),
                  pl.BlockSpec((1, tn), lambda j: (0, j))],
        out_specs=pl.BlockSpec((m, tn), lambda j: (0, j)),
        out_shape=jax.ShapeDtypeStruct((m, n), F32),
        compiler_params=_cparams("parallel"),
        name="adaln_mod",
    )(cvec, ada_w, ada_b.reshape(1, n))


def _norm_proj_kernel(x_ref, nw_ref, sc_ref, sh_ref, w_ref, wg_ref, o_ref, og_ref, h_ref):
    n_seq, _, rows, _ = o_ref.shape

    @pl.when(pl.program_id(1) == 0)
    def _():
        x = x_ref[...]
        y = x * lax.rsqrt(jnp.mean(x * x, axis=-1, keepdims=True) + EPS) * nw_ref[...]
        h_ref[...] = (y * (1.0 + sc_ref[0]) + sh_ref[0]).astype(BF16)
        og = jnp.dot(h_ref[...], wg_ref[...], preferred_element_type=F32)
        for s in range(n_seq):
            og_ref[s] = og[s * rows:(s + 1) * rows]

    acc =jnp.dot(h_ref[...], w_ref[...], preferred_element_type=F32)
    for s in range(n_seq):
        for k in range(o_ref.shape[1]):
            o_ref[s, k] = acc[s * rows:(s + 1) * rows, k * LANES:(k + 1) * LANES].astype(o_ref.dtype)


def _norm_proj(x, norm_w, scale, shift, w, w_gates, tm, tn):
    b, l, d = x.shape
    n = w.shape[1]
    tpb = max(l // tm, 1)
    spt = max(tm // l, 1)
    per_sample = scale.shape[0] != 1
    assert spt == 1 or not per_sample
    mod_idx = (lambda i, j: (i // tpb, 0, 0)) if per_sample else (lambda i, j: (0, 0, 0))
    return pl.pallas_call(
        _norm_proj_kernel,
        grid=(b * l // tm, n // tn),
        in_specs=[pl.BlockSpec((tm, d), lambda i, j: (i, 0)),
                  pl.BlockSpec((1, d), lambda i, j: (0, 0)),
                  pl.BlockSpec((1, 1, d), mod_idx),
                  pl.BlockSpec((1, 1, d), mod_idx),
                  pl.BlockSpec((d, tn), lambda i, j: (0, j)),
                  pl.BlockSpec((d, LANES), lambda i, j: (0, 0))],
        out_specs=[pl.BlockSpec((spt, tn // LANES, tm // spt, LANES), lambda i, j: (i // tpb, j, i % tpb, 0)),
                   pl.BlockSpec((spt, tm // spt, LANES), lambda i, j: (i // tpb, i % tpb, 0))],
        out_shape=[jax.ShapeDtypeStruct((b, n // LANES, l, LANES), BF16),
                   jax.ShapeDtypeStruct((b, l, LANES), F32)],
        scratch_shapes=[pltpu.VMEM((tm, d), BF16)],
        compiler_params=_cparams("parallel", "arbitrary"),
        name="norm_proj",
    )(x.reshape(b * l, d), norm_w.reshape(1, d), scale, shift, w, w_gates)


def _dn_gates_kernel(x_ref, alog_ref, dtb_ref, o_ref):
    l = x_ref.shape[1]
    lane = lax.broadcasted_iota(jnp.int32, (LANES, LANES), 1)
    row = lax.broadcasted_iota(jnp.int32, (LANES, LANES), 0)
    same_chunk = (row // DN_CHUNK) == (lane // DN_CHUNK)
    prefix_m = jnp.where(same_chunk & (lane <= row), 1.0, 0.0).astype(F32)
    suffix_m = jnp.where(same_chunk & (lane >= row), 1.0, 0.0).astype(F32)
    neg_a = -jnp.exp(alog_ref[...])
    dtb = dtb_ref[...]

    def tile(t, carry):
        rows = pl.ds(pl.multiple_of(t * LANES, LANES), LANES)
        x = x_ref[0, rows, :]
        beta = _sigmoid(x)
        z = x + dtb
        g = neg_a * (jnp.maximum(z, 0.0) + jnp.log1p(jnp.exp(-jnp.abs(z))))
        pre = jnp.dot(prefix_m, g, preferred_element_type=F32, precision=lax.Precision.HIGHEST)
        suf = jnp.dot(suffix_m, g, preferred_element_type=F32, precision=lax.Precision.HIGHEST)
        out = jnp.where(lane < 2 * N_HEADS, beta, jnp.where(lane < 3 * N_HEADS, pre, suf))
        o_ref[0, :, rows] = out.T[:4 * N_HEADS]
        return carry

    lax.fori_loop(0, l // LANES, tile, 0)


def _dn_gates(ba, alog_lanes, dtb_lanes):
    b, l, _ = ba.shape
    return pl.pallas_call(
        _dn_gates_kernel,
        grid=(b,),
        in_specs=[pl.BlockSpec((1, l, LANES), lambda i: (i, 0, 0)),
                  pl.BlockSpec((1, LANES), lambda i: (0, 0)),
                  pl.BlockSpec((1, LANES), lambda i: (0, 0))],
        out_specs=pl.BlockSpec((1, 4 * N_HEADS, l), lambda i: (i, 0, 0)),
        out_shape=jax.ShapeDtypeStruct((b, 4 * N_HEADS, l), F32),
        compiler_params=_cparams("parallel"),
        name="dn_gates",
    )(ba, alog_lanes, dtb_lanes)


def _dn_masks():
    row = lax.broadcasted_iota(jnp.int32, (LANES, LANES), 0)
    col = lax.broadcasted_iota(jnp.int32, (LANES, LANES), 1)
    fwd = row < DN_CHUNK
    same = (row // DN_CHUNK) == (col // DN_CHUNK)
    tril = same & ((fwd & (row >= col)) | (~fwd & (row <= col)))
    strict = tril & (row != col)
    return row, col, tril, strict


def _dn_prepare(h, n_rows, raw_refs, conv_refs, bg_ref, pad_ref, qkv_s, gate_s):
    gate_s[...] = jnp.zeros(gate_s.shape, F32)
    for d in range(2):
        gate_s[d, 0:1, :] = bg_ref[0, pl.ds(d * N_HEADS + h, 1), :]
        gate_s[d, 1:2, :] = bg_ref[0, pl.ds((2 + d) * N_HEADS + h, 1), :]
    n_tiles = n_rows // LANES
    zeros_halo = jnp.zeros((CONV_HALO, LANES), F32)
    for which in range(3):
        pad_ref[which, 0:CONV_HALO, :] = zeros_halo
        pad_ref[which, CONV_HALO + n_rows:2 * CONV_HALO + n_rows, :] = zeros_halo
        pad_ref[which, CONV_HALO:CONV_HALO + n_rows, :] = raw_refs[which][0, 0].astype(F32)
    taps = [conv_refs[which][0] for which in range(3)]

    def tile(t, carry):
        rows = pl.ds(pl.multiple_of(t * LANES, LANES), LANES)
        base = t * LANES + CONV_HALO - CONV_K // 2
        ys = []
        for which in range(3):
            acc = pad_ref[which, pl.ds(base, LANES), :] * taps[which][0:1, :]
            for j in range(1, CONV_K):
                acc = acc + pad_ref[which, pl.ds(base + j, LANES), :] * taps[which][j:j + 1, :]
            ys.append(_silu(acc))
        q, k, v = ys
        qkv_s[0, rows, :] = q * (lax.rsqrt(jnp.sum(q * q, axis=-1, keepdims=True) + EPS) * (HEAD_DIM ** -0.5))
        qkv_s[1, rows, :] = k * lax.rsqrt(jnp.sum(k * k, axis=-1, keepdims=True) + EPS)
        qkv_s[2, rows, :] = v
        return carry

    lax.fori_loop(0, n_tiles, tile, 0, unroll=2)


def _mm(x, y):
    return jnp.dot(x.astype(BF16), y.astype(BF16), preferred_element_type=F32)


def _unit_tri_solve(a_all, rhs_all, row, col, tick):
    blk = lambda m: (row // m) == (col // m)
    eye = jnp.where(row == col, 1.0, 0.0)
    ad = [jnp.where(blk(DN_SOLVE_BASE), a, 0.0) for a in a_all]
    t = [eye - x for x in ad]
    p = [_mm(x, x) for x in ad]
    tick()
    for _ in range(int(math.log2(DN_SOLVE_BASE)) - 2):
        tp = [_mm(jnp.concatenate([ti, pi], axis=0), pi) for ti, pi in zip(t, p)]
        tick()
        t = [ti + x[:LANES] for ti, x in zip(t, tp)]
        p = [x[LANES:] for x in tp]
    tp = [_mm(ti, pi) for ti, pi in zip(t, p)]
    tick()
    t = [ti + x for ti, x in zip(t, tp)]
    k = 2 * DN_SOLVE_BASE
    while k < DN_CHUNK:
        nk = [_mm(ti, jnp.where(blk(k) & ~blk(k // 2), a, 0.0)) for ti, a in zip(t, a_all)]
        tick()
        tn = [_mm(x, ti) for x, ti in zip(nk, t)]
        tick()
        t = [ti - x for ti, x in zip(t, tn)]
        k *= 2
    ny = [_mm(ti, jnp.concatenate([jnp.where(blk(k // 2), 0.0, a), r], axis=1))
          for ti, a, r in zip(t, a_all, rhs_all)]
    tick()
    ny2 = [_mm(x[:, :LANES], x[:, LANES:]) for x in ny]
    tick()
    return [x[:, LANES:] - z for x, z in zip(ny, ny2)]


def _dn_intra(steps, n, qkv_s, gate_s, lhs_s, c_s, o0_s, eg_s, with_out, slot0=0, tick=lambda: None):
    row, col, tril, strict = _dn_masks()
    is_f = row < DN_CHUNK
    nt = (((1,), (1,)), ((), ()))

    def load(i):
        ri = pl.ds(pl.multiple_of(i * DN_CHUNK, DN_CHUNK), DN_CHUNK)
        rj = pl.ds(pl.multiple_of((n - 1 - i) * DN_CHUNK, DN_CHUNK), DN_CHUNK)
        pair = lambda ref, a, b: jnp.concatenate([ref[a, ri, :], ref[b, rj, :]], axis=0)
        q2, k2, v2 = pair(qkv_s, 0, 0), pair(qkv_s, 1, 1), pair(qkv_s, 2, 2)
        j = n - 1 - i
        ga = gate_s[0, :, pl.ds(pl.multiple_of((i // 2) * LANES, LANES), LANES)]
        gb = gate_s[1, :, pl.ds(pl.multiple_of((j // 2) * LANES, LANES), LANES)]
        lo = lax.broadcasted_iota(jnp.int32, ga.shape, 1) < DN_CHUNK
        gates = jnp.where(i % 2 == 0, jnp.where(lo, ga, gb),
                          pltpu.roll(jnp.where(lo, gb, ga), DN_CHUNK, axis=1))
        g_row = jnp.broadcast_to(gates[1:2, :], (LANES, LANES))
        g2 = g_row.T
        b2 = jnp.broadcast_to(gates[0:1, :], (LANES, LANES)).T
        decay = jnp.exp(jnp.where(tril, g2 - g_row, -jnp.inf))
        return dict(q2=q2, k2=k2, v2=v2, b2=b2, g2=g2, decay=decay, kb2=k2 * b2, eg=jnp.exp(g2))

    st = [load(i) for i in steps]
    if with_out:
        kk = [lax.dot_general(jnp.concatenate([d["kb2"], d["q2"]], axis=0).astype(BF16), d["k2"].astype(BF16), nt,
                              preferred_element_type=F32) for d in st]
        a_qk = [jnp.where(tril, x[LANES:] * d["decay"], 0.0) for x, d in zip(kk, st)]
    else:
        kk = [lax.dot_general(d["kb2"].astype(BF16), d["k2"].astype(BF16), nt, preferred_element_type=F32)
              for d in st]
    tick()
    a = [jnp.where(strict, x[:LANES] * d["decay"], 0.0) for x, d in zip(kk, st)]
    sol = _unit_tri_solve(a, [jnp.concatenate([d["v2"] * d["b2"], d["kb2"] * d["eg"]], axis=1) for d in st],
                          row, col, tick)
    split = lambda x: [jnp.where(is_f, x, 0.0), jnp.where(is_f, 0.0, x)]
    kwu = []
    for d, x in zip(st, sol):
        g2 = d["g2"]
        gl_f, gl_b = g2[DN_CHUNK - 1:DN_CHUNK, :], g2[DN_CHUNK:DN_CHUNK + 1, :]
        gl = jnp.concatenate([jnp.broadcast_to(gl_f, (DN_CHUNK, LANES)),
                              jnp.broadcast_to(gl_b, (DN_CHUNK, LANES))], axis=0)
        kd2 = d["k2"] * jnp.exp(gl - g2)
        d["egl"] = jnp.exp(jnp.concatenate([gl_f, gl_b, jnp.zeros((6, LANES), F32)], axis=0))
        kwu.append(_mm(kd2.T, jnp.concatenate(split(x[:, LANES:]) + split(x[:, :LANES]), axis=1)))
    if with_out:
        awu = [_mm(x, y) for x, y in zip(a_qk, sol)]
    tick()
    for k, i in enumerate(steps):
        i = slot0 + i
        c_s[i, 0] = kwu[k][:, 2 * LANES:3 * LANES]
        c_s[i, 1] = kwu[k][:, 3 * LANES:]
        eg_s[i] = st[k]["egl"]
        lhs_s[i, 0, 0:LANES, :] = (-kwu[k][:, :LANES]).astype(BF16)
        lhs_s[i, 1, 0:LANES, :] = (-kwu[k][:, LANES:2 * LANES]).astype(BF16)
        if with_out:
            q_eff = st[k]["q2"] * st[k]["eg"] - awu[k][:, LANES:]
            o0_s[i] = awu[k][:, :LANES]
            lhs_s[i, 0, LANES:LANES + DN_CHUNK, :] = q_eff[:DN_CHUNK].astype(BF16)
            lhs_s[i, 1, LANES:LANES + DN_CHUNK, :] = q_eff[DN_CHUNK:].astype(BF16)


def _dn_recur(i, n, s_f, s_b, lhs_s, c_s, o0_s, eg_s, o_s, with_out, slot0=0):
    m = LANES + DN_CHUNK if with_out else LANES
    k = slot0 + i
    r_f = jnp.dot(lhs_s[k, 0, 0:m, :], s_f.astype(BF16), preferred_element_type=F32)
    r_b = jnp.dot(lhs_s[k, 1, 0:m, :], s_b.astype(BF16), preferred_element_type=F32)
    if with_out:
        o0 = o0_s[i]
        o_s[0, pl.ds(pl.multiple_of(i * DN_CHUNK, DN_CHUNK), DN_CHUNK), :] = r_f[LANES:] + o0[:DN_CHUNK]
        o_s[1, pl.ds(pl.multiple_of((n - 1 - i) * DN_CHUNK, DN_CHUNK), DN_CHUNK), :] = r_b[LANES:] + o0[DN_CHUNK:]
    eg = eg_s[k]
    s_f = s_f * jnp.broadcast_to(eg[0:1, :], (LANES, LANES)) + r_f[:LANES] + c_s[k, 0]
    s_b = s_b * jnp.broadcast_to(eg[1:2, :], (LANES, LANES)) + r_b[:LANES] + c_s[k, 1]
    return s_f, s_b


def _delta_kernel(q_ref, k_ref, v_ref, z_ref, cq_ref, ck_ref, cv_ref, bg_ref, cbg_ref,
                  wq_ref, wk_ref, wv_ref, nw_ref, o_ref,
                  pad_s, qkv_s, gate_s, cqkv_s, cgate_s, lhs_s, c_s, o0_s, eg_s, o_s):
    h = pl.program_id(1)
    l = q_ref.shape[2]
    lc = cq_ref.shape[2]
    n, nc = l // DN_CHUNK, lc // DN_CHUNK
    conv_refs = (wq_ref, wk_ref, wv_ref)
    stage = (lhs_s, c_s, o0_s, eg_s)

    _dn_prepare(h, lc, (cq_ref, ck_ref, cv_ref), conv_refs, cbg_ref, pad_s, cqkv_s, cgate_s)
    _dn_prepare(h, l, (q_ref, k_ref, v_ref), conv_refs, bg_ref, pad_s, qkv_s, gate_s)

    group = math.gcd(DN_GROUP, n)
    n_groups = n // group
    gc = math.gcd(DN_GROUP, nc)
    for g in range(nc // gc):
        _dn_intra([g * gc + k for k in range(gc)], nc, cqkv_s, cgate_s, *stage, with_out=False, slot0=n)
    state = [jnp.zeros((LANES, LANES), F32)] * 2

    def ctx_step(i):
        state[:] = _dn_recur(i, nc, state[0], state[1], *stage, o_s, with_out=False, slot0=n)

    def lat_step(i):
        state[:] = _dn_recur(i, n, state[0], state[1], *stage, o_s, with_out=True)

    def lat_group(g, scan_steps):
        pending = list(scan_steps)
        tick = lambda: pending.pop(0)() if pending else None
        _dn_intra([g * group + k for k in range(group)], n, qkv_s, gate_s, *stage, with_out=True, tick=tick)
        while pending:
            tick()

    lat_group(0, [functools.partial(ctx_step, i) for i in range(nc)])

    def pipelined(g, s):
        state[:] = s
        lat_group(g, [functools.partial(lat_step, (g - 1) * group + k) for k in range(group)])
        return tuple(state)

    state[:] = lax.fori_loop(1, n_groups, pipelined, tuple(state))

    nw = nw_ref[...]

    def out_tile(t):
        rows = pl.ds(t * LANES, LANES)
        o = o_s[0, rows, :] + o_s[1, rows, :]
        y = o * lax.rsqrt(jnp.mean(o * o, axis=-1, keepdims=True) + EPS) * nw
        o_ref[0, 0, rows, :] = (y * _silu(z_ref[0, 0, rows, :].astype(F32))).astype(o_ref.dtype)

    per_tile = LANES // DN_CHUNK
    final_at = lambda t: max(per_tile * t + per_tile - 1, n - 1 - per_tile * t)
    first = (n_groups - 1) * group
    ready = [t for t in range(l // LANES) if final_at(t) < first]
    for i in range(first, n):
        lat_step(i)
        ready += [t for t in range(l // LANES) if final_at(t) == i]
        if ready:
            out_tile(ready.pop(0))
    for t in ready:
        out_tile(t)


def _delta_branch(p_lat, p_ctx, bg, cbg, conv_taps, norm_w):
    b, _, l, _ = p_lat.shape
    lc = p_ctx.shape[2]
    n, nc = l // DN_CHUNK, lc // DN_CHUNK
    hb =lambda off: pl.BlockSpec((1, 1, l, LANES), lambda i, j, off=off: (i, off + j, 0, 0))
    cb = lambda off: pl.BlockSpec((1, 1, lc, LANES), lambda i, j, off=off: (i, off + j, 0, 0))
    tb = lambda off: pl.BlockSpec((1, 8, LANES), lambda i, j, off=off: (off + j, 0, 0))
    return pl.pallas_call(
        _delta_kernel,
        grid=(b, N_HEADS),
        in_specs=[hb(0), hb(N_HEADS), hb(2 * N_HEADS), hb(3 * N_HEADS),
                  cb(0), cb(N_HEADS), cb(2 * N_HEADS),
                  pl.BlockSpec((1, 4 * N_HEADS, l), lambda i, j: (i, 0, 0)),
                  pl.BlockSpec((1, 4 * N_HEADS, lc), lambda i, j: (i, 0, 0)),
                  tb(0), tb(N_HEADS), tb(2 * N_HEADS),
                  pl.BlockSpec((1, LANES), lambda i, j: (0, 0))],
        out_specs=pl.BlockSpec((1, 1, l, LANES), lambda i, j: (i, j, 0, 0)),
        out_shape=jax.ShapeDtypeStruct((b, N_HEADS, l, LANES), BF16),
        scratch_shapes=[pltpu.VMEM((3, l + 2 * CONV_HALO, LANES), F32),
                        pltpu.VMEM((3, l, LANES), F32),
                        pltpu.VMEM((2, 8, l), F32),
                        pltpu.VMEM((3, lc, LANES), F32),
                        pltpu.VMEM((2, 8, lc), F32),
                        pltpu.VMEM((n + nc, 2, LANES + DN_CHUNK, LANES), BF16),
                        pltpu.VMEM((n + nc, 2, LANES, LANES), F32),
                        pltpu.VMEM((n, LANES, LANES), F32),
                        pltpu.VMEM((n + nc, 8, LANES), F32),
                        pltpu.VMEM((2, l, LANES), F32)],
        compiler_params=_cparams("parallel", "arbitrary"),
        name="delta_branch",
    )(p_lat, p_lat, p_lat, p_lat, p_ctx, p_ctx, p_ctx, bg, cbg,
      conv_taps, conv_taps, conv_taps, norm_w.reshape(1, LANES))


def _rope_perm():
    quarter = HEAD_DIM // 4
    return np.concatenate([np.arange(quarter), 2 * quarter + np.arange(quarter),
                           quarter + np.arange(quarter), 3 * quarter + np.arange(quarter)])


def _rope(x, cos, sin):
    return x * cos + pltpu.roll(x, HEAD_DIM // 2, axis=1) * sin


def _head_rms(x, w):
    return x * lax.rsqrt(jnp.mean(x * x, axis=-1, keepdims=True) + EPS) * w


def _na_kernel(q_ref, k_ref, v_ref, ck_ref, cv_ref, qnw_ref, knw_ref, cos_ref, sin_ref, bias_ref, o_ref,
               q_s, k_s, ck_s):
    l = q_ref.shape[2]
    rows = l // GRID_W
    kh = min(NA_KH, rows)
    n_loc = kh * GRID_W
    qnw, knw = qnw_ref[...], knw_ref[...]

    def prep(t, c):
        r = pl.ds(pl.multiple_of(t * LANES, LANES), LANES)
        cos, sin = cos_ref[r, :], sin_ref[r, :]
        q = _rope(_head_rms(q_ref[0, 0, r, :].astype(F32), qnw), cos, sin) * (HEAD_DIM ** -0.5)
        q_s[r, :] = q.astype(BF16)
        k_s[r, :] = _rope(_head_rms(k_ref[0, 0, r, :].astype(F32), knw), cos, sin).astype(BF16)
        return c

    lax.fori_loop(0, l // LANES, prep, 0, unroll=2)
    ck_s[...] = _head_rms(ck_ref[0, 0].astype(F32), knw).astype(BF16)
    nt = (((1,), (1,)), ((), ()))

    group = math.gcd(NA_ROW_GROUP, rows)

    def row_group(g, c):
        rr = [g * group + k for k in range(group)]
        rs = [jnp.clip(r - kh // 2, 0, rows - kh) for r in rr]
        qr = [q_s[pl.ds(pl.multiple_of(r * GRID_W, GRID_W), GRID_W), :] for r in rr]
        kloc = [pl.ds(pl.multiple_of(x * GRID_W, GRID_W), n_loc) for x in rs]
        s_loc = [lax.dot_general(q, k_s[kl, :], nt, preferred_element_type=F32) for q, kl in zip(qr, kloc)]
        s_ctx = [lax.dot_general(q, ck_s[...], nt, preferred_element_type=F32) for q in qr]
        s_loc = [s + bias_ref[0, r - x] for s, r, x in zip(s_loc, rr, rs)]
        m = [jnp.maximum(jnp.max(sl, axis=-1, keepdims=True), jnp.max(sc, axis=-1, keepdims=True))
             for sl, sc in zip(s_loc, s_ctx)]
        p_loc = [jnp.exp(s - mi) for s, mi in zip(s_loc, m)]
        p_ctx = [jnp.exp(s - mi) for s, mi in zip(s_ctx, m)]
        denom = [jnp.sum(pl_, axis=-1, keepdims=True) + jnp.sum(pc, axis=-1, keepdims=True)
                 for pl_, pc in zip(p_loc, p_ctx)]
        o_loc = [jnp.dot(p.astype(BF16), v_ref[0, 0, kl, :], preferred_element_type=F32)
                 for p, kl in zip(p_loc, kloc)]
        o_ctx = [jnp.dot(p.astype(BF16), cv_ref[0, 0], preferred_element_type=F32) for p in p_ctx]
        for r, ol, oc, d in zip(rr, o_loc, o_ctx, denom):
            o_ref[0, 0, pl.ds(pl.multiple_of(r * GRID_W, GRID_W), GRID_W), :] = ((ol + oc) / d).astype(o_ref.dtype)
        return c

    lax.fori_loop(0, rows // group, row_group, 0)


def _na_tables(l):
    pos = jnp.arange(l)
    row = (pos // GRID_W).astype(F32)
    col = (pos % GRID_W).astype(F32)
    half = HEAD_DIM // 2
    inv_freq = ROPE_THETA ** (-jnp.arange(0, half, 2, dtype=F32) / half)
    ang_r = row[:, None] * inv_freq[None, :]
    ang_c = col[:, None] * inv_freq[None, :]
    cos = jnp.concatenate([jnp.cos(ang_r), jnp.cos(ang_r), jnp.cos(ang_c), jnp.cos(ang_c)], axis=-1)
    sin = jnp.concatenate([-jnp.sin(ang_r), jnp.sin(ang_r), -jnp.sin(ang_c), jnp.sin(ang_c)], axis=-1)
    perm = _rope_perm()
    return cos[:, perm], sin[:, perm]


def _na_bias_table(rpb, rows):
    kh = min(NA_KH, rows)
    t = np.arange(NA_KH)[:, None]
    i = np.arange(kh)[None, :]
    dr = np.clip(i - t + NA_KH - 1, 0, 2 * NA_KH - 2)
    q = np.arange(GRID_W)[:, None]
    kc = np.arange(GRID_W)[None, :]
    qstart = np.clip(q - NA_KW // 2, 0, GRID_W - NA_KW)
    in_win = (kc >= qstart) & (kc < qstart + NA_KW)
    dc = np.clip(kc - q + NA_KW - 1, 0, 2 * NA_KW - 2)
    pick_r = (dr[:, :, None] == np.arange(2 * NA_KH - 1)).astype(np.float32)
    pick_c = (np.arange(2 * NA_KW - 1)[:, None, None] == dc[None]).astype(np.float32)
    tab = jnp.einsum("tir,hrc,cqk->htqik", pick_r, rpb.astype(F32), pick_c, precision=lax.Precision.HIGHEST)
    tab = jnp.where(jnp.asarray(in_win)[None, None, :, None, :], tab, NEG_INF)
    return tab.reshape(rpb.shape[0], NA_KH, GRID_W, kh * GRID_W)


def _na_branch(p_lat, p_ctx, q_norm_w, k_norm_w, rpb):
    b, _, l, _ = p_lat.shape
    lc = p_ctx.shape[2]
    rows = l // GRID_W
    n_loc = min(NA_KH, rows) * GRID_W
    cos, sin = _na_tables(l)
    bias = _na_bias_table(rpb, rows)
    hb = lambda off: pl.BlockSpec((1, 1, l, LANES), lambda j, i, off=off: (i, off + j, 0, 0))
    cb = lambda off: pl.BlockSpec((1, 1, lc, LANES), lambda j, i, off=off: (i, off + j, 0, 0))
    const = lambda shape: pl.BlockSpec(shape, lambda j, i: (0,) * len(shape))
    return pl.pallas_call(
        _na_kernel,
        grid=(N_HEADS, b),
        in_specs=[hb(4 * N_HEADS), hb(5 * N_HEADS), hb(6 * N_HEADS), cb(3 * N_HEADS), cb(4 * N_HEADS),
                  const((1, LANES)), const((1, LANES)), const((l, LANES)), const((l, LANES)),
                  pl.BlockSpec((1, NA_KH, GRID_W, n_loc), lambda j, i: (j, 0, 0, 0))],
        out_specs=pl.BlockSpec((1, 1, l, LANES), lambda j, i: (i, j, 0, 0)),
        out_shape=jax.ShapeDtypeStruct((b, N_HEADS, l, LANES), BF16),
        scratch_shapes=[pltpu.VMEM((l, LANES), BF16), pltpu.VMEM((l, LANES), BF16),
                        pltpu.VMEM((lc, LANES), BF16)],
        compiler_params=_cparams("parallel", "arbitrary"),
        name="na_branch",
    )(p_lat, p_lat, p_lat, p_ctx, p_ctx, q_norm_w[_rope_perm()].reshape(1, LANES),
      k_norm_w[_rope_perm()].reshape(1, LANES), cos, sin, bias)


def _merge_kernel(a_ref, b_ref, wa_ref, wb_ref, ga_ref, gb_ref, o_ref, a_s, b_s):
    @pl.when(pl.program_id(1) == 0)
    def _():
        for k in range(N_HEADS):
            a_s[:, k * LANES:(k + 1) * LANES] = a_ref[0, k]
            b_s[:, k * LANES:(k + 1) * LANES] = b_ref[0, k]

    ya = jnp.dot(a_s[...], wa_ref[...], preferred_element_type=F32)
    yb = jnp.dot(b_s[...], wb_ref[...], preferred_element_type=F32)
    for k in range(ga_ref.shape[1]):
        cols = slice(k * LANES, (k + 1) * LANES)
        o_ref[:, cols] = (_sigmoid(ga_ref[0, k].astype(F32)) * ya[:, cols]
                          + _sigmoid(gb_ref[0, k].astype(F32)) * yb[:, cols]).astype(o_ref.dtype)


def _merge(dn_o, na_o, w_a, w_b, p_lat, tm=1024, tn=512):
    b, _, l, _ = dn_o.shape
    d = w_a.shape[1]
    tpb = l // tm
    nb = tn // LANES
    head_blk = pl.BlockSpec((1, N_HEADS, tm, LANES), lambda i, j: (i // tpb, 0, i % tpb, 0))
    gate_blk = lambda off: pl.BlockSpec((1, nb, tm, LANES),
                                        lambda i, j, off=off: (i // tpb, off // nb + j, i % tpb, 0))
    w_blk = pl.BlockSpec((N_HEADS * LANES, tn), lambda i, j: (0, j))
    return pl.pallas_call(
        _merge_kernel,
        grid=(b * tpb, d // tn),
        in_specs=[head_blk, head_blk, w_blk, w_blk, gate_blk(7 * N_HEADS), gate_blk(8 * N_HEADS)],
        out_specs=pl.BlockSpec((tm, tn), lambda i, j: (i, j)),
        out_shape=jax.ShapeDtypeStruct((b * l, d), BF16),
        scratch_shapes=[pltpu.VMEM((tm, N_HEADS * LANES), BF16), pltpu.VMEM((tm, N_HEADS * LANES), BF16)],
        compiler_params=_cparams("parallel", "arbitrary"),
        name="merge",
    )(dn_o, na_o, w_a, w_b, p_lat, p_lat)


def _out_kernel(y_ref, w_ref, x_ref, g1_ref, nw_ref, sc_ref, sh_ref, wr_hi_ref, wr_lo_ref, x1_ref, h2_ref, lg_ref):
    x1 = x_ref[...] + g1_ref[0] * jnp.dot(y_ref[...], w_ref[...], preferred_element_type=F32)
    x1_ref[...] = x1
    h2 = (x1 * lax.rsqrt(jnp.mean(x1 * x1, axis=-1, keepdims=True) + EPS) * nw_ref[...]
          * (1.0 + sc_ref[0]) + sh_ref[0])
    h_hi = h2.astype(BF16)
    h2_ref[...] = h_hi
    h_lo = (h2 - h_hi.astype(F32)).astype(BF16)
    lg_ref[...] = (jnp.dot(h_hi, wr_hi_ref[...], preferred_element_type=F32)
                   + (jnp.dot(h_hi, wr_lo_ref[...], preferred_element_type=F32)
                      + jnp.dot(h_lo, wr_hi_ref[...], preferred_element_type=F32)))


def _out_proj(y, w_out, x2d, g1, norm_w, scale, shift, w_router_pad, l, tm=256):
    m, d = x2d.shape
    tpb = l // tm
    wr_hi = w_router_pad.astype(BF16)
    wr_lo = (w_router_pad - wr_hi.astype(F32)).astype(BF16)
    row_blk = lambda: pl.BlockSpec((tm, d), lambda i: (i, 0))
    mod_blk = lambda: pl.BlockSpec((1, 1, d), lambda i: (i // tpb, 0, 0))
    return pl.pallas_call(
        _out_kernel,
        grid=(m // tm,),
        in_specs=[row_blk(), pl.BlockSpec((d, d), lambda i: (0, 0)), row_blk(), mod_blk(),
                  pl.BlockSpec((1, d), lambda i: (0, 0)), mod_blk(), mod_blk(),
                  pl.BlockSpec((d, LANES), lambda i: (0, 0)), pl.BlockSpec((d, LANES), lambda i: (0, 0))],
        out_specs=[row_blk(), row_blk(), pl.BlockSpec((tm, LANES), lambda i: (i, 0))],
        out_shape=[jax.ShapeDtypeStruct((m, d), F32), jax.ShapeDtypeStruct((m, d), BF16),
                   jax.ShapeDtypeStruct((m, LANES), F32)],
        compiler_params=_cparams("parallel"),
        name="out_proj",
    )(y, w_out, x2d, g1, norm_w.reshape(1, d), scale, shift, wr_hi, wr_lo)


def _route_kernel(lg_ref, slot_t_ref, gate_t_ref, slot_ref, aff_s, slot_s):
    l = lg_ref.shape[0]
    n_tiles = l // LANES
    cap = EC_CAPACITY_FACTOR * l // N_EXPERTS
    lane = lax.broadcasted_iota(jnp.int32, (LANES, LANES), 1)
    row = lax.broadcasted_iota(jnp.int32, (LANES, LANES), 0)

    for t in range(n_tiles):
        x = jnp.where(lane < N_EXPERTS, lg_ref[t * LANES:(t + 1) * LANES, :], -jnp.inf)
        e = jnp.exp(x - jnp.max(x, axis=-1, keepdims=True))
        aff = e / jnp.sum(e, axis=-1, keepdims=True)
        aff_s[:, t * LANES:(t + 1) * LANES] = aff.T[:N_EXPERTS]

    aff_t = aff_s[...]
    keys = pltpu.bitcast(aff_t, jnp.int32)
    count_ge = lambda thr: jnp.sum(jnp.where(keys >= thr, 1.0, 0.0), axis=1, keepdims=True)

    def bisect(_, c):
        lo, hi = c
        mid = lo + (hi - lo) // 2
        ok = count_ge(mid) >= cap
        return jnp.where(ok, mid, lo), jnp.where(ok, hi, mid)

    inf_bits = 0x7F800000
    thr, _ = lax.fori_loop(0, 31, bisect, (jnp.zeros((N_EXPERTS, 1), jnp.int32),
                                           jnp.full((N_EXPERTS, 1), inf_bits, jnp.int32)))
    gt, eq = keys > thr, keys == thr
    need = cap - jnp.sum(jnp.where(gt, 1.0, 0.0), axis=1, keepdims=True)
    before = jnp.where(row < lane, 1.0, 0.0).astype(BF16)

    def excl_prefix(flags):
        out, off = [], jnp.zeros((N_EXPERTS, 1), F32)
        ones = jnp.where(flags, 1.0, 0.0)
        for t in range(n_tiles):
            f = ones[:, t * LANES:(t + 1) * LANES]
            out.append(jnp.dot(f.astype(BF16), before, preferred_element_type=F32) + off)
            off = off + jnp.sum(f, axis=1, keepdims=True)
        return jnp.concatenate(out, axis=1)

    sel = gt | (eq & (excl_prefix(eq) < need))
    slot_f = jnp.where(sel, excl_prefix(sel), -1.0)
    slot_t_ref[0] = slot_f.astype(jnp.int32)
    gate_t_ref[0] = aff_t
    slot_s[...] = jnp.full(slot_s.shape, -1.0, F32)
    slot_s[0:N_EXPERTS, :] = slot_f
    for t in range(n_tiles):
        slot_ref[0, t * LANES:(t + 1) * LANES, :] = slot_s[:, t * LANES:(t + 1) * LANES].T.astype(jnp.int32)


def _route(logits, b, l):
    return pl.pallas_call(
        _route_kernel,
        grid=(b,),
        in_specs=[pl.BlockSpec((l, LANES), lambda i: (i, 0))],
        out_specs=[pl.BlockSpec((1, N_EXPERTS, l), lambda i: (i, 0, 0)),
                   pl.BlockSpec((1, N_EXPERTS, l), lambda i: (i, 0, 0)),
                   pl.BlockSpec((1, l, LANES), lambda i: (i, 0, 0))],
        out_shape=[jax.ShapeDtypeStruct((b, N_EXPERTS, l), jnp.int32),
                   jax.ShapeDtypeStruct((b, N_EXPERTS, l), F32),
                   jax.ShapeDtypeStruct((b, l, LANES), jnp.int32)],
        scratch_shapes=[pltpu.VMEM((N_EXPERTS, l), F32), pltpu.VMEM((LANES, l), F32)],
        compiler_params=_cparams("parallel"),
        name="route",
    )(logits)


def _ffn_kernel(h_ref, slot_t_ref, gate_t_ref, w1_ref, w3_ref, w2_ref, o_ref):
    e = pl.program_id(0)
    cap, l = o_ref.shape[2], h_ref.shape[1]
    hit = lax.broadcasted_iota(jnp.int32, (cap, l), 0) == slot_t_ref[0, pl.ds(e, 1), :]
    gate = jnp.sum(jnp.where(hit, gate_t_ref[0, pl.ds(e, 1), :], 0.0), axis=1, keepdims=True)
    x = jnp.dot(jnp.where(hit, 1.0, 0.0).astype(BF16), h_ref[0], preferred_element_type=F32).astype(BF16)
    h1 = jnp.dot(x, w1_ref[0], preferred_element_type=F32)
    h3 = jnp.dot(x, w3_ref[0], preferred_element_type=F32)
    hid = (_silu(h1) * h3).astype(BF16)
    o_ref[0, 0] = (jnp.dot(hid, w2_ref[0], preferred_element_type=F32) * gate).astype(o_ref.dtype)


def _expert_ffn(h2, slot_t, gate_t, w1, w3, w2, cap):
    b, l, d = h2.shape
    e, _, f = w1.shape
    return pl.pallas_call(
        _ffn_kernel,
        grid=(e, b),
        in_specs=[pl.BlockSpec((1, l, d), lambda j, i: (i, 0, 0)),
                  pl.BlockSpec((1, e, l), lambda j, i: (i, 0, 0)),
                  pl.BlockSpec((1, e, l), lambda j, i: (i, 0, 0)),
                  pl.BlockSpec((1, d, f), lambda j, i: (j, 0, 0)),
                  pl.BlockSpec((1, d, f), lambda j, i: (j, 0, 0)),
                  pl.BlockSpec((1, f, d), lambda j, i: (j, 0, 0))],
        out_specs=pl.BlockSpec((1, 1, cap, d), lambda j, i: (i, j, 0, 0)),
        out_shape=jax.ShapeDtypeStruct((b, e, cap, d), BF16),
        compiler_params=_cparams("parallel", "arbitrary"),
        name="expert_ffn",
    )(h2, slot_t, gate_t, w1, w3, w2)


def _combine_kernel(slot_ref, y_ref, x1_ref, g2_ref, o_ref):
    tm = slot_ref.shape[1]
    cap = y_ref.shape[1] // N_EXPERTS
    slot = slot_ref[0]
    j = lax.broadcasted_iota(jnp.int32, (tm, cap), 1)
    onehot = jnp.concatenate([jnp.where(slot[:, e:e + 1] == j, 1.0, 0.0).astype(BF16) for e in range(N_EXPERTS)],
                             axis=1)
    o_ref[0] = x1_ref[0] + g2_ref[0] * jnp.dot(onehot, y_ref[0], preferred_element_type=F32)


def _combine(slot, ye, x1, g2, tm=512, tn=1024):
    b, l, d = x1.shape
    ec = ye.shape[1]
    return pl.pallas_call(
        _combine_kernel,
        grid=(b, d // tn, l // tm),
        in_specs=[pl.BlockSpec((1, tm, LANES), lambda i, n, m: (i, m, 0)),
                  pl.BlockSpec((1, ec, tn), lambda i, n, m: (i, 0, n)),
                  pl.BlockSpec((1, tm, tn), lambda i, n, m: (i, m, n)),
                  pl.BlockSpec((1, 1, tn), lambda i, n, m: (i, 0, n))],
        out_specs=pl.BlockSpec((1, tm, tn), lambda i, n, m: (i, m, n)),
        out_shape=jax.ShapeDtypeStruct((b, l, d), F32),
        compiler_params=_cparams("parallel", "parallel", "arbitrary"),
        name="moe_combine",
    )(slot, ye, x1, g2)


def _layer(x, ctx, mod, mod_c, norm1_w, w_in, conv_w, a_log, dt_bias, dn_norm_w, q_norm_w, k_norm_w, rpb,
           w_a, w_b, w_out, norm2_w, w_router, w1, w3, w2):
    b, l, d = x.shape
    sh1, sc1, g1, sh2, sc2, g2 = [m[:, None, :] for m in jnp.split(mod, 6, axis=-1)]
    sh1c, sc1c = mod_c[None, None, :d], mod_c[None, None, d:2 * d]

    hd = N_HEADS * HEAD_DIM
    offs = np.cumsum([0, hd, hd, hd, hd, 2 * N_HEADS, 2 * N_HEADS, hd, hd, hd, d, d])
    col = lambda k: w_in[:, offs[k]:offs[k + 1]]
    perm = _rope_perm()
    rot = lambda k: col(k).reshape(d, N_HEADS, HEAD_DIM)[:, :, perm].reshape(d, hd)
    w_main = jnp.concatenate([col(0), col(1), col(2), col(3), rot(6), rot(7), col(8), col(9), col(10)],
                             axis=1).astype(BF16)
    w_ctx = jnp.concatenate([col(0), col(1), col(2), rot(7), col(8)], axis=1).astype(BF16)
    w_ba = jnp.pad(jnp.concatenate([col(4), col(5)], axis=1), ((0, 0), (0, LANES - 4 * N_HEADS))).astype(BF16)

    p_lat, ba = _norm_proj(x, norm1_w, sc1, sh1, w_main, w_ba, tm=1024, tn=1024)
    p_ctx, cba = _norm_proj(ctx, norm1_w, sc1c, sh1c, w_ctx, w_ba, tm=min(1024, b * ctx.shape[1]), tn=1024)

    lanes_pad = lambda v: jnp.pad(v.reshape(1, -1).astype(F32), ((0, 0), (2 * N_HEADS, LANES - 4 * N_HEADS)))
    alog_l, dtb_l = lanes_pad(a_log), lanes_pad(dt_bias)
    bg, cbg = _dn_gates(ba, alog_l, dtb_l), _dn_gates(cba, alog_l, dtb_l)
    taps = jnp.pad(conv_w.astype(F32), ((0, 8 - CONV_K), (0, 0))).reshape(8, 3 * N_HEADS, LANES).transpose(1, 0, 2)
    dn_o = _delta_branch(p_lat, p_ctx, bg, cbg, taps, dn_norm_w.astype(F32))
    na_o = _na_branch(p_lat, p_ctx, q_norm_w.astype(F32), k_norm_w.astype(F32), rpb)

    y = _merge(dn_o, na_o, w_a.astype(BF16), w_b.astype(BF16), p_lat)
    w_router_pad = jnp.pad(w_router.astype(F32), ((0, 0), (0, LANES - N_EXPERTS)))
    x1, h2, logits = _out_proj(y, w_out.astype(BF16), x.reshape(b * l, d), g1, norm2_w, sc2, sh2,
                               w_router_pad, l)

    cap = EC_CAPACITY_FACTOR * l // N_EXPERTS
    slot_t, gate_t, slot = _route(logits, b, l)
    ye = _expert_ffn(h2.reshape(b, l, d), slot_t, gate_t, w1.astype(BF16), w3.astype(BF16), w2.astype(BF16), cap)
    return _combine(slot, ye.reshape(b, N_EXPERTS * cap, d), x1.reshape(b, l, d), g2)


def kernel(x, c, ctx, c_ctx, ada_w, ada_b, norm1_w, w_in, conv_w, dn_a_log, dn_dt_bias, dn_norm_w,
           na_q_norm_w, na_k_norm_w, na_rpb, w_branch_a, w_branch_b, w_out, norm2_w, w_router,
           expert_w1, expert_w3, expert_w2):
    b = x.shape[0]
    depth = ada_w.shape[0]
    cvec = jnp.concatenate([c, c_ctx[None, :], jnp.zeros((16 - b - 1, c.shape[1]), c.dtype)], axis=0)
    for i in range(depth):
        mod_all = _adaln_mod(cvec, ada_w[i], ada_b[i])
        x = _layer(x, ctx, mod_all[:b], mod_all[b], norm1_w[i], w_in[i], conv_w[i], dn_a_log[i],
                   dn_dt_bias[i], dn_norm_w[i], na_q_norm_w[i], na_k_norm_w[i], na_rpb[i],
                   w_branch_a[i], w_branch_b[i], w_out[i], norm2_w[i], w_router[i],
                   expert_w1[i], expert_w3[i], expert_w2[i])
    return x
```

```python
import functools
import math

import numpy as np
import jax
import jax.numpy as jnp
from jax import lax
from jax.experimental import pallas as pl
from jax.experimental.pallas import tpu as pltpu

F32 = jnp.float32
BF16 = jnp.bfloat16

EPS = 1e-6
NEG_INF = -1e30
LANES = 128
GRID_W = 64
N_HEADS = 16
HEAD_DIM = 128
CONV_K = 5
CONV_HALO = 16
DN_CHUNK = 64
DN_GROUP = 8
DN_SOLVE_BASE = 16
NA_KH = 8
NA_KW = 16
NA_ROW_GROUP = 8
ROPE_THETA = 10000.0
N_EXPERTS = 16
EC_CAPACITY_FACTOR = 2
VMEM_LIMIT = 56 * 1024 * 1024


def _cparams(*sem):
    return pltpu.CompilerParams(dimension_semantics=sem, vmem_limit_bytes=VMEM_LIMIT)


def _sigmoid(x):
    return 1.0 / (1.0 + jnp.exp(-x))


def _silu(x):
    return x * _sigmoid(x)


def _mod_kernel(c_ref, w_ref, b_ref, o_ref):
    s = _silu(c_ref[...])
    o_ref[...] = jnp.dot(s, w_ref[...], preferred_element_type=F32,
                         precision=lax.Precision.HIGHEST) + b_ref[...]


def _adaln_mod(cvec, ada_w, ada_b, tn=1024):
    m, d = cvec.shape
    n = ada_w.shape[1]
    return pl.pallas_call(
        _mod_kernel,
        grid=(n // tn,),
        in_specs=[pl.BlockSpec((m, d), lambda j: (0, 0)),
                  pl.BlockSpec((d, tn), lambda j: (0, j)),
                  pl.BlockSpec((1, tn), lambda j: (0, j))],
        out_specs=pl.BlockSpec((m, tn), lambda j: (0, j)),
        out_shape=jax.ShapeDtypeStruct((m, n), F32),
        compiler_params=_cparams("parallel"),
        name="adaln_mod",
    )(cvec, ada_w, ada_b.reshape(1, n))


def _norm_proj_kernel(x_ref, nw_ref, sc_ref, sh_ref, w_ref, wg_ref, o_ref, og_ref, h_ref):
    n_seq, _, rows, _ = o_ref.shape

    @pl.when(pl.program_id(1) == 0)
    def _():
        x = x_ref[...]
        y = x * lax.rsqrt(jnp.mean(x * x, axis=-1, keepdims=True) + EPS) * nw_ref[...]
        h_ref[...] = (y * (1.0 + sc_ref[0]) + sh_ref[0]).astype(BF16)
        og = jnp.dot(h_ref[...], wg_ref[...], preferred_element_type=F32)
        for s in range(n_seq):
            og_ref[s] = og[s * rows:(s + 1) * rows]

    acc = jnp.dot(h_ref[...], w_ref[...], preferred_element_type=F32)
    for s in range(n_seq):
        for k in range(o_ref.shape[1]):
            o_ref[s, k] = acc[s * rows:(s + 1) * rows, k * LANES:(k + 1) * LANES].astype(o_ref.dtype)


def _norm_proj(x, norm_w, scale, shift, w, w_gates, tm, tn):
    b, l, d = x.shape
    n = w.shape[1]
    tpb = max(l // tm, 1)
    spt = max(tm // l, 1)
    per_sample = scale.shape[0] != 1
    assert spt == 1 or not per_sample
    mod_idx = (lambda i, j: (i // tpb, 0, 0)) if per_sample else (lambda i, j: (0, 0, 0))
    return pl.pallas_call(
        _norm_proj_kernel,
        grid=(b * l // tm, n // tn),
        in_specs=[pl.BlockSpec((tm, d), lambda i, j: (i, 0)),
                  pl.BlockSpec((1, d), lambda i, j: (0, 0)),
                  pl.BlockSpec((1, 1, d), mod_idx),
                  pl.BlockSpec((1, 1, d), mod_idx),
                  pl.BlockSpec((d, tn), lambda i, j: (0, j)),
                  pl.BlockSpec((d, LANES), lambda i, j: (0, 0))],
        out_specs=[pl.BlockSpec((spt, tn // LANES, tm // spt, LANES), lambda i, j: (i // tpb, j, i % tpb, 0)),
                   pl.BlockSpec((spt, tm // spt, LANES), lambda i, j: (i // tpb, i % tpb, 0))],
        out_shape=[jax.ShapeDtypeStruct((b, n // LANES, l, LANES), BF16),
                   jax.ShapeDtypeStruct((b, l, LANES), F32)],
        scratch_shapes=[pltpu.VMEM((tm, d), BF16)],
        compiler_params=_cparams("parallel", "arbitrary"),
        name="norm_proj",
    )(x.reshape(b * l, d), norm_w.reshape(1, d), scale, shift, w, w_gates)


def _dn_gates_kernel(x_ref, alog_ref, dtb_ref, o_ref):
    l = x_ref.shape[1]
    lane = lax.broadcasted_iota(jnp.int32, (LANES, LANES), 1)
    row = lax.broadcasted_iota(jnp.int32, (LANES, LANES), 0)
    same_chunk = (row // DN_CHUNK) == (lane // DN_CHUNK)
    prefix_m = jnp.where(same_chunk & (lane <= row), 1.0, 0.0).astype(F32)
    suffix_m = jnp.where(same_chunk & (lane >= row), 1.0, 0.0).astype(F32)
    neg_a = -jnp.exp(alog_ref[...])
    dtb = dtb_ref[...]

    def tile(t, carry):
        rows = pl.ds(pl.multiple_of(t * LANES, LANES), LANES)
        x = x_ref[0, rows, :]
        beta = _sigmoid(x)
        z = x + dtb
        g = neg_a * (jnp.maximum(z, 0.0) + jnp.log1p(jnp.exp(-jnp.abs(z))))
        pre = jnp.dot(prefix_m, g, preferred_element_type=F32, precision=lax.Precision.HIGHEST)
        suf = jnp.dot(suffix_m, g, preferred_element_type=F32, precision=lax.Precision.HIGHEST)
        out = jnp.where(lane < 2 * N_HEADS, beta, jnp.where(lane < 3 * N_HEADS, pre, suf))
        o_ref[0, :, rows] = out.T[:4 * N_HEADS]
        return carry

    lax.fori_loop(0, l // LANES, tile, 0)


def _dn_gates(ba, alog_lanes, dtb_lanes):
    b, l, _ = ba.shape
    return pl.pallas_call(
        _dn_gates_kernel,
        grid=(b,),
        in_specs=[pl.BlockSpec((1, l, LANES), lambda i: (i, 0, 0)),
                  pl.BlockSpec((1, LANES), lambda i: (0, 0)),
                  pl.BlockSpec((1, LANES), lambda i: (0, 0))],
        out_specs=pl.BlockSpec((1, 4 * N_HEADS, l), lambda i: (i, 0, 0)),
        out_shape=jax.ShapeDtypeStruct((b, 4 * N_HEADS, l), F32),
        compiler_params=_cparams("parallel"),
        name="dn_gates",
    )(ba, alog_lanes, dtb_lanes)


def _dn_masks():
    row = lax.broadcasted_iota(jnp.int32, (LANES, LANES), 0)
    col = lax.broadcasted_iota(jnp.int32, (LANES, LANES), 1)
    fwd = row < DN_CHUNK
    same = (row // DN_CHUNK) == (col // DN_CHUNK)
    tril = same & ((fwd & (row >= col)) | (~fwd & (row <= col)))
    strict = tril & (row != col)
    return row, col, tril, strict


def _dn_prepare(h, n_rows, raw_refs, conv_refs, bg_ref, pad_ref, qkv_s, gate_s, defer=False):
    gate_s[...] = jnp.zeros(gate_s.shape, F32)
    for d in range(2):
        gate_s[d, 0:1, :] = bg_ref[0, pl.ds(d * N_HEADS + h, 1), :]
        gate_s[d, 1:2, :] = bg_ref[0, pl.ds((2 + d) * N_HEADS + h, 1), :]
    n_tiles = n_rows // LANES
    zeros_halo = jnp.zeros((CONV_HALO, LANES), F32)
    for which in range(3):
        pad_ref[which, 0:CONV_HALO, :] = zeros_halo
        pad_ref[which, CONV_HALO + n_rows:2 * CONV_HALO + n_rows, :] = zeros_halo
        pad_ref[which, CONV_HALO:CONV_HALO + n_rows, :] = raw_refs[which][0, 0].astype(F32)
    taps = [conv_refs[which][0] for which in range(3)]

    def tile(t, carry=None):
        rows = pl.ds(pl.multiple_of(t * LANES, LANES), LANES)
        base = t * LANES + CONV_HALO - CONV_K // 2
        ys = []
        for which in range(3):
            acc = pad_ref[which, pl.ds(base, LANES), :] * taps[which][0:1, :]
            for j in range(1, CONV_K):
                acc = acc + pad_ref[which, pl.ds(base + j, LANES), :] * taps[which][j:j + 1, :]
            ys.append(_silu(acc))
        q, k, v = ys
        qkv_s[0, rows, :] = q * (lax.rsqrt(jnp.sum(q * q, axis=-1, keepdims=True) + EPS) * (HEAD_DIM ** -0.5))
        qkv_s[1, rows, :] = k * lax.rsqrt(jnp.sum(k * k, axis=-1, keepdims=True) + EPS)
        qkv_s[2, rows, :] = v
        return carry

    if defer:
        return [functools.partial(tile, t) for t in range(n_tiles)]
    lax.fori_loop(0, n_tiles, tile, 0, unroll=2)
    return []


def _mm(x, y):
    return jnp.dot(x.astype(BF16), y.astype(BF16), preferred_element_type=F32)


def _unit_tri_solve(a_all, rhs_all, row, col, tick):
    blk = lambda m: (row // m) == (col // m)
    eye = jnp.where(row == col, 1.0, 0.0)
    ad = [jnp.where(blk(DN_SOLVE_BASE), a, 0.0) for a in a_all]
    t = [eye - x for x in ad]
    p = [_mm(x, x) for x in ad]
    tick()
    for _ in range(int(math.log2(DN_SOLVE_BASE)) - 2):
        tp = [_mm(jnp.concatenate([ti, pi], axis=0), pi) for ti, pi in zip(t, p)]
        tick()
        t = [ti + x[:LANES] for ti, x in zip(t, tp)]
        p = [x[LANES:] for x in tp]
    tp = [_mm(ti, pi) for ti, pi in zip(t, p)]
    tick()
    t = [ti + x for ti, x in zip(t, tp)]
    k = 2 * DN_SOLVE_BASE
    while k < DN_CHUNK:
        nk = [_mm(ti, jnp.where(blk(k) & ~blk(k // 2), a, 0.0)) for ti, a in zip(t, a_all)]
        tick()
        tn = [_mm(x, ti) for x, ti in zip(nk, t)]
        tick()
        t = [ti - x for ti, x in zip(t, tn)]
        k *= 2
    ny = [_mm(ti, jnp.concatenate([jnp.where(blk(k // 2), 0.0, a), r], axis=1))
          for ti, a, r in zip(t, a_all, rhs_all)]
    tick()
    ny2 = [_mm(x[:, :LANES], x[:, LANES:]) for x in ny]
    tick()
    return [x[:, LANES:] - z for x, z in zip(ny, ny2)]


def _dn_intra(steps, n, qkv_s, gate_s, lhs_s, c_s, o0_s, eg_s, with_out, slot0=0, tick=lambda: None):
    row, col, tril, strict = _dn_masks()
    is_f = row < DN_CHUNK
    nt = (((1,), (1,)), ((), ()))

    def load(i):
        ri = pl.ds(pl.multiple_of(i * DN_CHUNK, DN_CHUNK), DN_CHUNK)
        rj = pl.ds(pl.multiple_of((n - 1 - i) * DN_CHUNK, DN_CHUNK), DN_CHUNK)
        pair = lambda ref, a, b: jnp.concatenate([ref[a, ri, :], ref[b, rj, :]], axis=0)
        q2, k2, v2 = pair(qkv_s, 0, 0), pair(qkv_s, 1, 1), pair(qkv_s, 2, 2)
        j = n - 1 - i
        ga = gate_s[0, :, pl.ds(pl.multiple_of((i // 2) * LANES, LANES), LANES)]
        gb = gate_s[1, :, pl.ds(pl.multiple_of((j // 2) * LANES, LANES), LANES)]
        lo = lax.broadcasted_iota(jnp.int32, ga.shape, 1) < DN_CHUNK
        gates = jnp.where(i % 2 == 0, jnp.where(lo, ga, gb),
                          pltpu.roll(jnp.where(lo, gb, ga), DN_CHUNK, axis=1))
        g_row = jnp.broadcast_to(gates[1:2, :], (LANES, LANES))
        g2 = g_row.T
        b2 = jnp.broadcast_to(gates[0:1, :], (LANES, LANES)).T
        decay = jnp.exp(jnp.where(tril, g2 - g_row, -jnp.inf))
        return dict(q2=q2, k2=k2, v2=v2, b2=b2, g2=g2, decay=decay, kb2=k2 * b2, eg=jnp.exp(g2))

    st = [load(i) for i in steps]
    if with_out:
        kk = [lax.dot_general(jnp.concatenate([d["kb2"], d["q2"]], axis=0).astype(BF16), d["k2"].astype(BF16), nt,
                              preferred_element_type=F32) for d in st]
        a_qk = [jnp.where(tril, x[LANES:] * d["decay"], 0.0) for x, d in zip(kk, st)]
    else:
        kk = [lax.dot_general(d["kb2"].astype(BF16), d["k2"].astype(BF16), nt, preferred_element_type=F32)
              for d in st]
    tick()
    a = [jnp.where(strict, x[:LANES] * d["decay"], 0.0) for x, d in zip(kk, st)]
    sol = _unit_tri_solve(a, [jnp.concatenate([d["v2"] * d["b2"], d["kb2"] * d["eg"]], axis=1) for d in st],
                          row, col, tick)
    split = lambda x: [jnp.where(is_f, x, 0.0), jnp.where(is_f, 0.0, x)]
    kwu = []
    for d, x in zip(st, sol):
        g2 = d["g2"]
        gl_f, gl_b = g2[DN_CHUNK - 1:DN_CHUNK, :], g2[DN_CHUNK:DN_CHUNK + 1, :]
        gl = jnp.concatenate([jnp.broadcast_to(gl_f, (DN_CHUNK, LANES)),
                              jnp.broadcast_to(gl_b, (DN_CHUNK, LANES))], axis=0)
        kd2 = d["k2"] * jnp.exp(gl - g2)
        d["egl"] = jnp.exp(jnp.concatenate([gl_f, gl_b, jnp.zeros((6, LANES), F32)], axis=0))
        kwu.append(_mm(kd2.T, jnp.concatenate(split(x[:, LANES:]) + split(x[:, :LANES]), axis=1)))
    if with_out:
        awu = [_mm(x, y) for x, y in zip(a_qk, sol)]
    tick()
    for k, i in enumerate(steps):
        i = slot0 + i
        c_s[i, 0] = kwu[k][:, 2 * LANES:3 * LANES]
        c_s[i, 1] = kwu[k][:, 3 * LANES:]
        eg_s[i] = st[k]["egl"]
        lhs_s[i, 0, 0:LANES, :] = (-kwu[k][:, :LANES]).astype(BF16)
        lhs_s[i, 1, 0:LANES, :] = (-kwu[k][:, LANES:2 * LANES]).astype(BF16)
        if with_out:
            q_eff = st[k]["q2"] * st[k]["eg"] - awu[k][:, LANES:]
            o0_s[i] = awu[k][:, :LANES]
            lhs_s[i, 0, LANES:LANES + DN_CHUNK, :] = q_eff[:DN_CHUNK].astype(BF16)
            lhs_s[i, 1, LANES:LANES + DN_CHUNK, :] = q_eff[DN_CHUNK:].astype(BF16)


def _dn_recur(i, n, s_f, s_b, lhs_s, c_s, o0_s, eg_s, o_s, with_out, slot0=0):
    m = LANES + DN_CHUNK if with_out else LANES
    k = slot0 + i
    r_f = jnp.dot(lhs_s[k, 0, 0:m, :], s_f.astype(BF16), preferred_element_type=F32)
    r_b = jnp.dot(lhs_s[k, 1, 0:m, :], s_b.astype(BF16), preferred_element_type=F32)
    if with_out:
        o0 = o0_s[i]
        o_s[0, pl.ds(pl.multiple_of(i * DN_CHUNK, DN_CHUNK), DN_CHUNK), :] = r_f[LANES:] + o0[:DN_CHUNK]
        o_s[1, pl.ds(pl.multiple_of((n - 1 - i) * DN_CHUNK, DN_CHUNK), DN_CHUNK), :] = r_b[LANES:] + o0[DN_CHUNK:]
    eg = eg_s[k]
    s_f = s_f * jnp.broadcast_to(eg[0:1, :], (LANES, LANES)) + r_f[:LANES] + c_s[k, 0]
    s_b = s_b * jnp.broadcast_to(eg[1:2, :], (LANES, LANES)) + r_b[:LANES] + c_s[k, 1]
    return s_f, s_b


def _delta_kernel(q_ref, k_ref, v_ref, z_ref, cq_ref, ck_ref, cv_ref, bg_ref, cbg_ref,
                  wq_ref, wk_ref, wv_ref, nw_ref, o_ref,
                  pad_s, qkv_s, gate_s, cqkv_s, cgate_s, lhs_s, c_s, o0_s, eg_s, o_s):
    h = pl.program_id(1)
    l = q_ref.shape[2]
    lc = cq_ref.shape[2]
    n, nc = l // DN_CHUNK, lc // DN_CHUNK
    conv_refs = (wq_ref, wk_ref, wv_ref)
    stage = (lhs_s, c_s, o0_s, eg_s)

    _dn_prepare(h, lc, (cq_ref, ck_ref, cv_ref), conv_refs, cbg_ref, pad_s, cqkv_s, cgate_s)
    lat_tiles = _dn_prepare(h, l, (q_ref, k_ref, v_ref), conv_refs, bg_ref, pad_s, qkv_s, gate_s, defer=True)

    group = math.gcd(DN_GROUP, n)
    n_groups = n // group
    gc = math.gcd(DN_GROUP, nc)

    def two_tiles():
        for _ in range(2):
            if lat_tiles:
                lat_tiles.pop(0)()

    for g in range(nc // gc):
        _dn_intra([g * gc + k for k in range(gc)], nc, cqkv_s, cgate_s, *stage, with_out=False, slot0=n,
                  tick=two_tiles)
    while lat_tiles:
        two_tiles()
    state = [jnp.zeros((LANES, LANES), F32)] * 2

    def ctx_step(i):
        state[:] = _dn_recur(i, nc, state[0], state[1], *stage, o_s, with_out=False, slot0=n)

    def lat_step(i):
        state[:] = _dn_recur(i, n, state[0], state[1], *stage, o_s, with_out=True)

    def lat_group(g, scan_steps):
        pending = list(scan_steps)
        tick = lambda: pending.pop(0)() if pending else None
        _dn_intra([g * group + k for k in range(group)], n, qkv_s, gate_s, *stage, with_out=True, tick=tick)
        while pending:
            tick()

    lat_group(0, [functools.partial(ctx_step, i) for i in range(nc)])

    def pipelined(g, s):
        state[:] = s
        lat_group(g, [functools.partial(lat_step, (g - 1) * group + k) for k in range(group)])
        return tuple(state)

    state[:] = lax.fori_loop(1, n_groups, pipelined, tuple(state))

    nw = nw_ref[...]

    def out_tile(t):
        rows = pl.ds(t * LANES, LANES)
        o = o_s[0, rows, :] + o_s[1, rows, :]
        y = o * lax.rsqrt(jnp.mean(o * o, axis=-1, keepdims=True) + EPS) * nw
        o_ref[0, 0, rows, :] = (y * _silu(z_ref[0, 0, rows, :].astype(F32))).astype(o_ref.dtype)

    per_tile = LANES // DN_CHUNK
    final_at = lambda t: max(per_tile * t + per_tile - 1, n - 1 - per_tile * t)
    first = (n_groups - 1) * group
    ready = [t for t in range(l // LANES) if final_at(t) < first]
    for i in range(first, n):
        lat_step(i)
        ready += [t for t in range(l // LANES) if final_at(t) == i]
        if ready:
            out_tile(ready.pop(0))
    for t in ready:
        out_tile(t)


def _delta_branch(p_lat, p_ctx, bg, cbg, conv_taps, norm_w):
    b, _, l, _ = p_lat.shape
    lc = p_ctx.shape[2]
    n, nc = l // DN_CHUNK, lc // DN_CHUNK
    hb =lambda off: pl.BlockSpec((1, 1, l, LANES), lambda i, j, off=off: (i, off + j, 0, 0))
    cb = lambda off: pl.BlockSpec((1, 1, lc, LANES), lambda i, j, off=off: (i, off + j, 0, 0))
    tb = lambda off: pl.BlockSpec((1, 8, LANES), lambda i, j, off=off: (off + j, 0, 0))
    return pl.pallas_call(
        _delta_kernel,
        grid=(b, N_HEADS),
        in_specs=[hb(0), hb(N_HEADS), hb(2 * N_HEADS), hb(3 * N_HEADS),
                  cb(0), cb(N_HEADS), cb(2 * N_HEADS),
                  pl.BlockSpec((1, 4 * N_HEADS, l), lambda i, j: (i, 0, 0)),
                  pl.BlockSpec((1, 4 * N_HEADS, lc), lambda i, j: (i, 0, 0)),
                  tb(0), tb(N_HEADS), tb(2 * N_HEADS),
                  pl.BlockSpec((1, LANES), lambda i, j: (0, 0))],
        out_specs=pl.BlockSpec((1, 1, l, LANES), lambda i, j: (i, j, 0, 0)),
        out_shape=jax.ShapeDtypeStruct((b, N_HEADS, l, LANES), BF16),
        scratch_shapes=[pltpu.VMEM((3, l + 2 * CONV_HALO, LANES), F32),
                        pltpu.VMEM((3, l, LANES), F32),
                        pltpu.VMEM((2, 8, l), F32),
                        pltpu.VMEM((3, lc, LANES), F32),
                        pltpu.VMEM((2, 8, lc), F32),
                        pltpu.VMEM((n + nc, 2, LANES + DN_CHUNK, LANES), BF16),
                        pltpu.VMEM((n + nc, 2, LANES, LANES), F32),
                        pltpu.VMEM((n, LANES, LANES), F32),
                        pltpu.VMEM((n + nc, 8, LANES), F32),
                        pltpu.VMEM((2, l, LANES), F32)],
        compiler_params=_cparams("parallel", "arbitrary"),
        name="delta_branch",
    )(p_lat, p_lat, p_lat, p_lat, p_ctx, p_ctx, p_ctx, bg, cbg,
      conv_taps, conv_taps, conv_taps, norm_w.reshape(1, LANES))


def _rope_perm():
    quarter = HEAD_DIM // 4
    return np.concatenate([np.arange(quarter), 2 * quarter + np.arange(quarter),
                           quarter + np.arange(quarter), 3 * quarter + np.arange(quarter)])


def _rope(x, cos, sin):
    return x * cos + pltpu.roll(x, HEAD_DIM // 2, axis=1) * sin


def _head_rms(x, w):
    return x * lax.rsqrt(jnp.mean(x * x, axis=-1, keepdims=True) + EPS) * w


def _na_kernel(q_ref, k_ref, v_ref, ck_ref, cv_ref, qnw_ref, knw_ref, cos_ref, sin_ref, bias_ref, o_ref,
               q_s, k_s, ck_s):
    l = q_ref.shape[2]
    rows = l // GRID_W
    kh = min(NA_KH, rows)
    n_loc = kh * GRID_W
    qnw, knw = qnw_ref[...], knw_ref[...]

    def prep(t):
        r = pl.ds(t * LANES, LANES)
        cos, sin = cos_ref[r, :], sin_ref[r, :]
        q = _rope(_head_rms(q_ref[0, 0, r, :].astype(F32), qnw), cos, sin) * (HEAD_DIM ** -0.5)
        q_s[r, :] = q.astype(BF16)
        k_s[r, :] = _rope(_head_rms(k_ref[0, 0, r, :].astype(F32), knw), cos, sin).astype(BF16)

    ck_s[...] = _head_rms(ck_ref[0, 0].astype(F32), knw).astype(BF16)
    nt = (((1,), (1,)), ((), ()))
    group = math.gcd(NA_ROW_GROUP, rows)
    window_start = lambda r: min(max(r - kh // 2, 0), rows - kh)
    tiles_needed = lambda g: -(-(window_start(g * group + group - 1) + kh) * GRID_W // LANES)

    def row_group(g, pending):
        tick = lambda: pending.pop(0)() if pending else None
        rr = [g * group + k for k in range(group)]
        rs = [window_start(r) for r in rr]
        qr = [q_s[pl.ds(r * GRID_W, GRID_W), :] for r in rr]
        kloc = [pl.ds(x * GRID_W, n_loc) for x in rs]
        s_loc = [lax.dot_general(q, k_s[kl, :], nt, preferred_element_type=F32) for q, kl in zip(qr, kloc)]
        tick()
        s_ctx = [lax.dot_general(q, ck_s[...], nt, preferred_element_type=F32) for q in qr]
        tick()
        s_loc = [s + bias_ref[0, r - x] for s, r, x in zip(s_loc, rr, rs)]
        m = [jnp.maximum(jnp.max(sl, axis=-1, keepdims=True), jnp.max(sc, axis=-1, keepdims=True))
             for sl, sc in zip(s_loc, s_ctx)]
        tick()
        p_loc = [jnp.exp(s - mi) for s, mi in zip(s_loc, m)]
        p_ctx = [jnp.exp(s - mi) for s, mi in zip(s_ctx, m)]
        tick()
        denom = [jnp.sum(pl_, axis=-1, keepdims=True) + jnp.sum(pc, axis=-1, keepdims=True)
                 for pl_, pc in zip(p_loc, p_ctx)]
        o_loc = [jnp.dot(p.astype(BF16), v_ref[0, 0, kl, :], preferred_element_type=F32)
                 for p, kl in zip(p_loc, kloc)]
        o_ctx = [jnp.dot(p.astype(BF16), cv_ref[0, 0], preferred_element_type=F32) for p in p_ctx]
        while pending:
            tick()
        for r, ol, oc, d in zip(rr, o_loc, o_ctx, denom):
            o_ref[0, 0, pl.ds(r * GRID_W, GRID_W), :] = ((ol + oc) / d).astype(o_ref.dtype)

    n_groups = rows // group
    for t in range(tiles_needed(0)):
        prep(t)
    for g in range(n_groups):
        nxt = range(tiles_needed(g), tiles_needed(g + 1)) if g + 1 < n_groups else ()
        row_group(g, [functools.partial(prep, t) for t in nxt])


def _na_tables(l):
    pos = jnp.arange(l)
    row = (pos // GRID_W).astype(F32)
    col = (pos % GRID_W).astype(F32)
    half = HEAD_DIM // 2
    inv_freq = ROPE_THETA ** (-jnp.arange(0, half, 2, dtype=F32) / half)
    ang_r = row[:, None] * inv_freq[None, :]
    ang_c = col[:, None] * inv_freq[None, :]
    cos = jnp.concatenate([jnp.cos(ang_r), jnp.cos(ang_r), jnp.cos(ang_c), jnp.cos(ang_c)], axis=-1)
    sin = jnp.concatenate([-jnp.sin(ang_r), jnp.sin(ang_r), -jnp.sin(ang_c), jnp.sin(ang_c)], axis=-1)
    perm = _rope_perm()
    return cos[:, perm], sin[:, perm]


def _na_bias_table(rpb, rows):
    kh = min(NA_KH, rows)
    t = np.arange(NA_KH)[:, None]
    i = np.arange(kh)[None, :]
    dr = np.clip(i - t + NA_KH - 1, 0, 2 * NA_KH - 2)
    q = np.arange(GRID_W)[:, None]
    kc = np.arange(GRID_W)[None, :]
    qstart = np.clip(q - NA_KW // 2, 0, GRID_W - NA_KW)
    in_win = (kc >= qstart) & (kc < qstart + NA_KW)
    dc = np.clip(kc - q + NA_KW - 1, 0, 2 * NA_KW - 2)
    pick_r = (dr[:, :, None] == np.arange(2 * NA_KH - 1)).astype(np.float32)
    pick_c = (np.arange(2 * NA_KW - 1)[:, None, None] == dc[None]).astype(np.float32)
    tab = jnp.einsum("tir,hrc,cqk->htqik", pick_r, rpb.astype(F32), pick_c, precision=lax.Precision.HIGHEST)
    tab = jnp.where(jnp.asarray(in_win)[None, None, :, None, :], tab, NEG_INF)
    return tab.reshape(rpb.shape[0], NA_KH, GRID_W, kh * GRID_W)


def _na_branch(p_lat, p_ctx, q_norm_w, k_norm_w, rpb):
    b, _, l, _ = p_lat.shape
    lc = p_ctx.shape[2]
    rows = l // GRID_W
    n_loc = min(NA_KH, rows) * GRID_W
    cos, sin = _na_tables(l)
    bias = _na_bias_table(rpb, rows)
    hb = lambda off: pl.BlockSpec((1, 1, l, LANES), lambda j, i, off=off: (i, off + j, 0, 0))
    cb = lambda off: pl.BlockSpec((1, 1, lc, LANES), lambda j, i, off=off: (i, off + j, 0, 0))
    const = lambda shape: pl.BlockSpec(shape, lambda j, i: (0,) * len(shape))
    return pl.pallas_call(
        _na_kernel,
        grid=(N_HEADS, b),
        in_specs=[hb(4 * N_HEADS), hb(5 * N_HEADS), hb(6 * N_HEADS), cb(3 * N_HEADS), cb(4 * N_HEADS),
                  const((1, LANES)), const((1, LANES)), const((l, LANES)), const((l, LANES)),
                  pl.BlockSpec((1, NA_KH, GRID_W, n_loc), lambda j, i: (j, 0, 0, 0))],
        out_specs=pl.BlockSpec((1, 1, l, LANES), lambda j, i: (i, j, 0, 0)),
        out_shape=jax.ShapeDtypeStruct((b, N_HEADS, l, LANES), BF16),
        scratch_shapes=[pltpu.VMEM((l, LANES), BF16), pltpu.VMEM((l, LANES), BF16),
                        pltpu.VMEM((lc, LANES), BF16)],
        compiler_params=_cparams("parallel", "arbitrary"),
        name="na_branch",
    )(p_lat, p_lat, p_lat, p_ctx, p_ctx, q_norm_w[_rope_perm()].reshape(1, LANES),
      k_norm_w[_rope_perm()].reshape(1, LANES), cos, sin, bias)


def _merge_kernel(a_ref, b_ref, wa_ref, wb_ref, ga_ref, gb_ref, o_ref, a_s, b_s):
    @pl.when(pl.program_id(1) == 0)
    def _():
        for k in range(N_HEADS):
            a_s[:, k * LANES:(k + 1) * LANES] = a_ref[0, k]
            b_s[:, k * LANES:(k + 1) * LANES] = b_ref[0, k]

    ya = jnp.dot(a_s[...], wa_ref[...], preferred_element_type=F32)
    yb = jnp.dot(b_s[...], wb_ref[...], preferred_element_type=F32)
    for k in range(ga_ref.shape[1]):
        cols = slice(k * LANES, (k + 1) * LANES)
        o_ref[:, cols] = (_sigmoid(ga_ref[0, k].astype(F32)) * ya[:, cols]
                          + _sigmoid(gb_ref[0, k].astype(F32)) * yb[:, cols]).astype(o_ref.dtype)


def _merge(dn_o, na_o, w_a, w_b, p_lat, tm=1024, tn=512):
    b, _, l, _ = dn_o.shape
    d = w_a.shape[1]
    tpb = l // tm
    nb = tn // LANES
    head_blk = pl.BlockSpec((1, N_HEADS, tm, LANES), lambda i, j: (i // tpb, 0, i % tpb, 0))
    gate_blk = lambda off: pl.BlockSpec((1, nb, tm, LANES),
                                        lambda i, j, off=off: (i // tpb, off // nb + j, i % tpb, 0))
    w_blk = pl.BlockSpec((N_HEADS * LANES, tn), lambda i, j: (0, j))
    return pl.pallas_call(
        _merge_kernel,
        grid=(b * tpb, d // tn),
        in_specs=[head_blk, head_blk, w_blk, w_blk, gate_blk(7 * N_HEADS), gate_blk(8 * N_HEADS)],
        out_specs=pl.BlockSpec((tm, tn), lambda i, j: (i, j)),
        out_shape=jax.ShapeDtypeStruct((b * l, d), BF16),
        scratch_shapes=[pltpu.VMEM((tm, N_HEADS * LANES), BF16), pltpu.VMEM((tm, N_HEADS * LANES), BF16)],
        compiler_params=_cparams("parallel", "arbitrary"),
        name="merge",
    )(dn_o, na_o, w_a, w_b, p_lat, p_lat)


def _out_kernel(y_ref, w_ref, x_ref, g1_ref, nw_ref, sc_ref, sh_ref, wr_hi_ref, wr_lo_ref, x1_ref, h2_ref, lg_ref):
    x1 = x_ref[...] + g1_ref[0] * jnp.dot(y_ref[...], w_ref[...], preferred_element_type=F32)
    x1_ref[...] = x1
    h2 = (x1 * lax.rsqrt(jnp.mean(x1 * x1, axis=-1, keepdims=True) + EPS) * nw_ref[...]
          * (1.0 + sc_ref[0]) + sh_ref[0])
    h_hi = h2.astype(BF16)
    h2_ref[...] = h_hi
    h_lo = (h2 - h_hi.astype(F32)).astype(BF16)
    lg_ref[...] = (jnp.dot(h_hi, wr_hi_ref[...], preferred_element_type=F32)
                   + (jnp.dot(h_hi, wr_lo_ref[...], preferred_element_type=F32)
                      + jnp.dot(h_lo, wr_hi_ref[...], preferred_element_type=F32)))


def _out_proj(y, w_out, x2d, g1, norm_w, scale, shift, w_router_pad, l, tm=256):
    m, d = x2d.shape
    tpb = l // tm
    wr_hi = w_router_pad.astype(BF16)
    wr_lo = (w_router_pad - wr_hi.astype(F32)).astype(BF16)
    row_blk = lambda: pl.BlockSpec((tm, d), lambda i: (i, 0))
    mod_blk = lambda: pl.BlockSpec((1, 1, d), lambda i: (i // tpb, 0, 0))
    return pl.pallas_call(
        _out_kernel,
        grid=(m // tm,),
        in_specs=[row_blk(), pl.BlockSpec((d, d), lambda i: (0, 0)), row_blk(), mod_blk(),
                  pl.BlockSpec((1, d), lambda i: (0, 0)), mod_blk(), mod_blk(),
                  pl.BlockSpec((d, LANES), lambda i: (0, 0)), pl.BlockSpec((d, LANES), lambda i: (0, 0))],
        out_specs=[row_blk(), row_blk(), pl.BlockSpec((tm, LANES), lambda i: (i, 0))],
        out_shape=[jax.ShapeDtypeStruct((m, d), F32), jax.ShapeDtypeStruct((m, d), BF16),
                   jax.ShapeDtypeStruct((m, LANES), F32)],
        compiler_params=_cparams("parallel"),
        name="out_proj",
    )(y, w_out, x2d, g1, norm_w.reshape(1, d), scale, shift, wr_hi, wr_lo)


def _route_kernel(lg_ref, slot_t_ref, gate_t_ref, slot_ref, aff_s, slot_s):
    l = lg_ref.shape[0]
    n_tiles = l // LANES
    cap = EC_CAPACITY_FACTOR * l // N_EXPERTS
    lane = lax.broadcasted_iota(jnp.int32, (LANES, LANES), 1)
    row = lax.broadcasted_iota(jnp.int32, (LANES, LANES), 0)

    for t in range(n_tiles):
        x = jnp.where(lane < N_EXPERTS, lg_ref[t * LANES:(t + 1) * LANES, :], -jnp.inf)
        e = jnp.exp(x - jnp.max(x, axis=-1, keepdims=True))
        aff = e / jnp.sum(e, axis=-1, keepdims=True)
        aff_s[:, t * LANES:(t + 1) * LANES] = aff.T[:N_EXPERTS]

    aff_t = aff_s[...]
    keys = pltpu.bitcast(aff_t, jnp.int32)
    count_ge = lambda thr: jnp.sum(jnp.where(keys >= thr, 1.0, 0.0), axis=1, keepdims=True)

    def bisect(_, c):
        lo, hi = c
        mid = lo + (hi - lo) // 2
        ok = count_ge(mid) >= cap
        return jnp.where(ok, mid, lo), jnp.where(ok, hi, mid)

    inf_bits = 0x7F800000
    thr, _ = lax.fori_loop(0, 31, bisect, (jnp.zeros((N_EXPERTS, 1), jnp.int32),
                                           jnp.full((N_EXPERTS, 1), inf_bits, jnp.int32)))
    gt, eq = keys > thr, keys == thr
    need = cap - jnp.sum(jnp.where(gt, 1.0, 0.0), axis=1, keepdims=True)
    before = jnp.where(row < lane, 1.0, 0.0).astype(BF16)

    def excl_prefix(flags):
        out, off = [], jnp.zeros((N_EXPERTS, 1), F32)
        ones = jnp.where(flags, 1.0, 0.0)
        for t in range(n_tiles):
            f = ones[:, t * LANES:(t + 1) * LANES]
            out.append(jnp.dot(f.astype(BF16), before, preferred_element_type=F32) + off)
            off = off + jnp.sum(f, axis=1, keepdims=True)
        return jnp.concatenate(out, axis=1)

    sel = gt | (eq & (excl_prefix(eq) < need))
    slot_f = jnp.where(sel, excl_prefix(sel), -1.0)
    slot_t_ref[0] = slot_f.astype(jnp.int32)
    gate_t_ref[0] = aff_t
    slot_s[...] = jnp.full(slot_s.shape, -1.0, F32)
    slot_s[0:N_EXPERTS, :] = slot_f
    for t in range(n_tiles):
        slot_ref[0, t * LANES:(t + 1) * LANES, :] = slot_s[:, t * LANES:(t + 1) * LANES].T.astype(jnp.int32)


def _route(logits, b, l):
    return pl.pallas_call(
        _route_kernel,
        grid=(b,),
        in_specs=[pl.BlockSpec((l, LANES), lambda i: (i, 0))],
        out_specs=[pl.BlockSpec((1, N_EXPERTS, l), lambda i: (i, 0, 0)),
                   pl.BlockSpec((1, N_EXPERTS, l), lambda i: (i, 0, 0)),
                   pl.BlockSpec((1, l, LANES), lambda i: (i, 0, 0))],
        out_shape=[jax.ShapeDtypeStruct((b, N_EXPERTS, l), jnp.int32),
                   jax.ShapeDtypeStruct((b, N_EXPERTS, l), F32),
                   jax.ShapeDtypeStruct((b, l, LANES), jnp.int32)],
        scratch_shapes=[pltpu.VMEM((N_EXPERTS, l), F32), pltpu.VMEM((LANES, l), F32)],
        compiler_params=_cparams("parallel"),
        name="route",
    )(logits)


def _ffn_kernel(h_ref, slot_t_ref, gate_t_ref, w1_ref, w3_ref, w2_ref, o_ref):
    e = pl.program_id(0)
    cap, l = o_ref.shape[2], h_ref.shape[1]
    hit = lax.broadcasted_iota(jnp.int32, (cap, l), 0) == slot_t_ref[0, pl.ds(e, 1), :]
    gate = jnp.sum(jnp.where(hit, gate_t_ref[0, pl.ds(e, 1), :], 0.0), axis=1, keepdims=True)
    x = jnp.dot(jnp.where(hit, 1.0, 0.0).astype(BF16), h_ref[0], preferred_element_type=F32).astype(BF16)
    h1 = jnp.dot(x, w1_ref[0], preferred_element_type=F32)
    h3 = jnp.dot(x, w3_ref[0], preferred_element_type=F32)
    hid = (_silu(h1) * h3).astype(BF16)
    o_ref[0, 0] = (jnp.dot(hid, w2_ref[0], preferred_element_type=F32) * gate).astype(o_ref.dtype)


def _expert_ffn(h2, slot_t, gate_t, w1, w3, w2, cap):
    b, l, d = h2.shape
    e, _, f = w1.shape
    return pl.pallas_call(
        _ffn_kernel,
        grid=(e, b),
        in_specs=[pl.BlockSpec((1, l, d), lambda j, i: (i, 0, 0)),
                  pl.BlockSpec((1, e, l), lambda j, i: (i, 0, 0)),
                  pl.BlockSpec((1, e, l), lambda j, i: (i, 0, 0)),
                  pl.BlockSpec((1, d, f), lambda j, i: (j, 0, 0)),
                  pl.BlockSpec((1, d, f), lambda j, i: (j, 0, 0)),
                  pl.BlockSpec((1, f, d), lambda j, i: (j, 0, 0))],
        out_specs=pl.BlockSpec((1, 1, cap, d), lambda j, i: (i, j, 0, 0)),
        out_shape=jax.ShapeDtypeStruct((b, e, cap, d), BF16),
        compiler_params=_cparams("parallel", "arbitrary"),
        name="expert_ffn",
    )(h2, slot_t, gate_t, w1, w3, w2)


def _combine_kernel(slot_ref, y_ref, x1_ref, g2_ref, o_ref):
    tm = slot_ref.shape[1]
    cap = y_ref.shape[1] // N_EXPERTS
    slot = slot_ref[0]
    j = lax.broadcasted_iota(jnp.int32, (tm, cap), 1)
    onehot = jnp.concatenate([jnp.where(slot[:, e:e + 1] == j, 1.0, 0.0).astype(BF16) for e in range(N_EXPERTS)],
                             axis=1)
    o_ref[0] = x1_ref[0] + g2_ref[0] * jnp.dot(onehot, y_ref[0], preferred_element_type=F32)


def _combine(slot, ye, x1, g2, tm=512, tn=1024):
    b, l, d = x1.shape
    ec = ye.shape[1]
    return pl.pallas_call(
        _combine_kernel,
        grid=(b, d // tn, l // tm),
        in_specs=[pl.BlockSpec((1, tm, LANES), lambda i, n, m: (i, m, 0)),
                  pl.BlockSpec((1, ec, tn), lambda i, n, m: (i, 0, n)),
                  pl.BlockSpec((1, tm, tn), lambda i, n, m: (i, m, n)),
                  pl.BlockSpec((1, 1, tn), lambda i, n, m: (i, 0, n))],
        out_specs=pl.BlockSpec((1, tm, tn), lambda i, n, m: (i, m, n)),
        out_shape=jax.ShapeDtypeStruct((b, l, d), F32),
        compiler_params=_cparams("parallel", "parallel", "arbitrary"),
        name="moe_combine",
    )(slot, ye, x1, g2)


def _layer(x, ctx, mod, mod_c, norm1_w, w_in, conv_w, a_log, dt_bias, dn_norm_w, q_norm_w, k_norm_w, rpb,
           w_a, w_b, w_out, norm2_w, w_router, w1, w3, w2):
    b, l, d = x.shape
    sh1, sc1, g1, sh2, sc2, g2 = [m[:, None, :] for m in jnp.split(mod, 6, axis=-1)]
    sh1c, sc1c = mod_c[None, None, :d], mod_c[None, None, d:2 * d]

    hd = N_HEADS * HEAD_DIM
    offs = np.cumsum([0, hd, hd, hd, hd, 2 * N_HEADS, 2 * N_HEADS, hd, hd, hd, d, d])
    w_in = w_in.astype(BF16)
    cols = lambda k0, k1: w_in[:, offs[k0]:offs[k1]]
    perm = _rope_perm()
    rot = lambda k0, k1: cols(k0, k1).reshape(d, -1, HEAD_DIM)[:, :, perm].reshape(d, -1)
    w_main = jnp.concatenate([cols(0, 4), rot(6, 8), cols(8, 11)], axis=1)
    w_ctx = jnp.concatenate([cols(0, 3), rot(7, 8), cols(8, 9)], axis=1)
    w_ba = jnp.pad(cols(4, 6), ((0, 0), (0, LANES - 4 * N_HEADS)))

    p_lat, ba = _norm_proj(x, norm1_w, sc1, sh1, w_main, w_ba, tm=1024, tn=1024)
    p_ctx, cba = _norm_proj(ctx, norm1_w, sc1c, sh1c, w_ctx, w_ba, tm=min(1024, b * ctx.shape[1]), tn=1024)

    lanes_pad = lambda v: jnp.pad(v.reshape(1, -1).astype(F32), ((0, 0), (2 * N_HEADS, LANES - 4 * N_HEADS)))
    alog_l, dtb_l = lanes_pad(a_log), lanes_pad(dt_bias)
    bg, cbg = _dn_gates(ba, alog_l, dtb_l), _dn_gates(cba, alog_l, dtb_l)
    taps = jnp.pad(conv_w.astype(F32), ((0, 8 - CONV_K), (0, 0))).reshape(8, 3 * N_HEADS, LANES).transpose(1, 0, 2)
    dn_o = _delta_branch(p_lat, p_ctx, bg, cbg, taps, dn_norm_w.astype(F32))
    na_o = _na_branch(p_lat, p_ctx, q_norm_w.astype(F32), k_norm_w.astype(F32), rpb)

    y = _merge(dn_o, na_o, w_a.astype(BF16), w_b.astype(BF16), p_lat)
    w_router_pad = jnp.pad(w_router.astype(F32), ((0, 0), (0, LANES - N_EXPERTS)))
    x1, h2, logits = _out_proj(y, w_out.astype(BF16), x.reshape(b * l, d), g1, norm2_w, sc2, sh2,
                               w_router_pad, l)

    cap = EC_CAPACITY_FACTOR * l // N_EXPERTS
    slot_t, gate_t, slot = _route(logits, b, l)
    ye = _expert_ffn(h2.reshape(b, l, d), slot_t, gate_t, w1.astype(BF16), w3.astype(BF16), w2.astype(BF16), cap)
    return _combine(slot, ye.reshape(b, N_EXPERTS * cap, d), x1.reshape(b, l, d), g2)


def kernel(x, c, ctx, c_ctx, ada_w, ada_b, norm1_w, w_in, conv_w, dn_a_log, dn_dt_bias, dn_norm_w,
           na_q_norm_w, na_k_norm_w, na_rpb, w_branch_a, w_branch_b, w_out, norm2_w, w_router,
           expert_w1, expert_w3, expert_w2):
    b = x.shape[0]
    depth = ada_w.shape[0]
    cvec = jnp.concatenate([c, c_ctx[None, :], jnp.zeros((16 - b - 1, c.shape[1]), c.dtype)], axis=0)
    for i in range(depth):
        mod_all = _adaln_mod(cvec, ada_w[i], ada_b[i])
        x = _layer(x, ctx, mod_all[:b], mod_all[b], norm1_w[i], w_in[i], conv_w[i], dn_a_log[i],
                   dn_dt_bias[i], dn_norm_w[i], na_q_norm_w[i], na_k_norm_w[i], na_rpb[i],
                   w_branch_a[i], w_branch_b[i], w_out[i], norm2_w[i], w_router[i],
                   expert_w1[i], expert_w3[i], expert_w2[i])
    return x
```

```python
import functools
import math

import numpy as np
import jax
import jax.numpy as jnp
from jax import lax
from jax.experimental import pallas as pl
from jax.experimental.pallas import tpu as pltpu

F32 = jnp.float32
BF16 = jnp.bfloat16

EPS = 1e-6
NEG_INF = -1e30
LANES = 128
GRID_W = 64
N_HEADS = 16
HEAD_DIM = 128
CONV_K = 5
CONV_HALO = 16
DN_CHUNK = 64
DN_GROUP = 8
DN_SOLVE_BASE = 16
NA_KH = 8
NA_KW = 16
NA_ROW_GROUP = 8
ROPE_THETA = 10000.0
N_EXPERTS = 16
EC_CAPACITY_FACTOR = 2
VMEM_LIMIT = 56 * 1024 * 1024


def _cparams(*sem):
    return pltpu.CompilerParams(dimension_semantics=sem, vmem_limit_bytes=VMEM_LIMIT)


def _sigmoid(x):
    return 1.0 / (1.0 + jnp.exp(-x))


def _silu(x):
    return x * _sigmoid(x)


def _mod_kernel(c_ref, w_ref, b_ref, o_ref):
    s = _silu(c_ref[...])
    o_ref[...] = jnp.dot(s, w_ref[...], preferred_element_type=F32,
                         precision=lax.Precision.HIGHEST) + b_ref[...]


def _adaln_mod(cvec, ada_w, ada_b, tn=1024):
    m, d = cvec.shape
    n = ada_w.shape[1]
    return pl.pallas_call(
        _mod_kernel,
        grid=(n // tn,),
        in_specs=[pl.BlockSpec((m, d), lambda j: (0, 0)),
                  pl.BlockSpec((d, tn), lambda j: (0, j)),
                  pl.BlockSpec((1, tn), lambda j: (0, j))],
        out_specs=pl.BlockSpec((m, tn), lambda j: (0, j)),
        out_shape=jax.ShapeDtypeStruct((m, n), F32),
        compiler_params=_cparams("parallel"),
        name="adaln_mod",
    )(cvec, ada_w, ada_b.reshape(1, n))


PACK_BLK = 2 * LANES


def _pack_kernel(a_ref, b_ref, o_ref, *, gap_blk, n_main, rot_end, gap_w):
    j = pl.program_id(0)
    shifted = lambda: jnp.concatenate([a_ref[:, gap_w:], b_ref[:, :gap_w]], axis=1)

    def relabel(t):
        q = HEAD_DIM // 4
        heads = [t[:, k * HEAD_DIM:(k + 1) * HEAD_DIM] for k in range(PACK_BLK // HEAD_DIM)]
        return jnp.concatenate([jnp.concatenate([hx[:, 0:q], hx[:, 2 * q:3 * q], hx[:, q:2 * q], hx[:, 3 * q:]], axis=1)
                                for hx in heads], axis=1)

    @pl.when(j < gap_blk)
    def _():
        o_ref[...] = a_ref[...].astype(o_ref.dtype)

    @pl.when((j >= gap_blk) & (j < rot_end))
    def _():
        o_ref[...] = relabel(shifted()).astype(o_ref.dtype)

    @pl.when((j >= rot_end) & (j < n_main))
    def _():
        o_ref[...] = shifted().astype(o_ref.dtype)

    @pl.when(j == n_main)
    def _():
        o_ref[...] = jnp.concatenate([a_ref[:, :gap_w], jnp.zeros((a_ref.shape[0], PACK_BLK - gap_w), F32)],
                                     axis=1).astype(o_ref.dtype)


def _pack_w_in(w_in, gap_start, gap_w, rot_start, rot_stop):
    d, n = w_in.shape
    assert gap_start % PACK_BLK == 0 and (n - gap_w) % PACK_BLK == 0 and rot_start == gap_start
    gap_blk, n_main = gap_start // PACK_BLK, (n - gap_w) // PACK_BLK
    kern = functools.partial(_pack_kernel, gap_blk=gap_blk, n_main=n_main, rot_end=rot_stop // PACK_BLK, gap_w=gap_w)
    return pl.pallas_call(
        kern,
        grid=(n_main + 1,),
        in_specs=[pl.BlockSpec((d, PACK_BLK), lambda j: (0, jnp.where(j == n_main, gap_blk, j))),
                  pl.BlockSpec((d, PACK_BLK), lambda j: (0, jnp.minimum(j + 1, n_main)))],
        out_specs=pl.BlockSpec((d, PACK_BLK), lambda j: (0, j)),
        out_shape=jax.ShapeDtypeStruct((d, (n_main + 1) * PACK_BLK), BF16),
        compiler_params=_cparams("parallel"),
        name="pack_w_in",
    )(w_in, w_in)


def _norm_proj_kernel(x_ref, nw_ref, sc_ref, sh_ref, w_ref, wg_ref, o_ref, og_ref, h_ref):
    n_seq, _, rows, _ = o_ref.shape

    @pl.when(pl.program_id(1) == 0)
    def _():
        x = x_ref[...]
        y = x * lax.rsqrt(jnp.mean(x * x, axis=-1, keepdims=True) + EPS) * nw_ref[...]
        h_ref[...] = (y * (1.0 + sc_ref[0]) + sh_ref[0]).astype(BF16)
        og = jnp.dot(h_ref[...], wg_ref[...], preferred_element_type=F32)
        for s in range(n_seq):
            og_ref[s] = og[s * rows:(s + 1) * rows]

    acc = jnp.dot(h_ref[...], w_ref[...], preferred_element_type=F32)
    for s in range(n_seq):
        for k in range(o_ref.shape[1]):
            o_ref[s, k] = acc[s * rows:(s + 1) * rows, k * LANES:(k + 1) * LANES].astype(o_ref.dtype)


def _norm_proj(x, norm_w, scale, shift, w, tm, tn, n_tiles, col_tile, gates_col):
    b, l, d = x.shape
    n = n_tiles * tn
    tpb = max(l // tm, 1)
    spt = max(tm // l, 1)
    per_sample = scale.shape[0] != 1
    assert spt == 1 or not per_sample
    mod_idx = (lambda i, j: (i // tpb, 0, 0)) if per_sample else (lambda i, j: (0, 0, 0))
    return pl.pallas_call(
        _norm_proj_kernel,
        grid=(b * l // tm, n // tn),
        in_specs=[pl.BlockSpec((tm, d), lambda i, j: (i, 0)),
                  pl.BlockSpec((1, d), lambda i, j: (0, 0)),
                  pl.BlockSpec((1, 1, d), mod_idx),
                  pl.BlockSpec((1, 1, d), mod_idx),
                  pl.BlockSpec((d, tn), lambda i, j: (0, col_tile(j))),
                  pl.BlockSpec((d, LANES), lambda i, j: (0, gates_col // LANES))],
        out_specs=[pl.BlockSpec((spt, tn // LANES, tm // spt, LANES), lambda i, j: (i // tpb, j, i % tpb, 0)),
                   pl.BlockSpec((spt, tm // spt, LANES), lambda i, j: (i // tpb, i % tpb, 0))],
        out_shape=[jax.ShapeDtypeStruct((b, n // LANES, l, LANES), BF16),
                   jax.ShapeDtypeStruct((b, l, LANES), F32)],
        scratch_shapes=[pltpu.VMEM((tm, d), BF16)],
        compiler_params=_cparams("parallel", "arbitrary"),
        name="norm_proj",
    )(x.reshape(b * l, d), norm_w.reshape(1, d), scale, shift, w, w)


def _dn_gates_kernel(x_ref, alog_ref, dtb_ref, o_ref):
    l = x_ref.shape[1]
    lane = lax.broadcasted_iota(jnp.int32, (LANES, LANES), 1)
    row = lax.broadcasted_iota(jnp.int32, (LANES, LANES), 0)
    same_chunk = (row // DN_CHUNK) == (lane // DN_CHUNK)
    prefix_m = jnp.where(same_chunk & (lane <= row), 1.0, 0.0).astype(F32)
    suffix_m = jnp.where(same_chunk & (lane >= row), 1.0, 0.0).astype(F32)
    neg_a = -jnp.exp(alog_ref[...])
    dtb = dtb_ref[...]

    def tile(t, carry):
        rows = pl.ds(pl.multiple_of(t * LANES, LANES), LANES)
        x = x_ref[0, rows, :]
        beta = _sigmoid(x)
        z = x + dtb
        g = neg_a * (jnp.maximum(z, 0.0) + jnp.log1p(jnp.exp(-jnp.abs(z))))
        pre = jnp.dot(prefix_m, g, preferred_element_type=F32, precision=lax.Precision.HIGHEST)
        suf = jnp.dot(suffix_m, g, preferred_element_type=F32, precision=lax.Precision.HIGHEST)
        out = jnp.where(lane < 2 * N_HEADS, beta, jnp.where(lane < 3 * N_HEADS, pre, suf))
        o_ref[0, :, rows] = out.T[:4 * N_HEADS]
        return carry

    lax.fori_loop(0, l // LANES, tile, 0)


def _dn_gates(ba, alog_lanes, dtb_lanes):
    b, l, _ = ba.shape
    return pl.pallas_call(
        _dn_gates_kernel,
        grid=(b,),
        in_specs=[pl.BlockSpec((1, l, LANES), lambda i: (i, 0, 0)),
                  pl.BlockSpec((1, LANES), lambda i: (0, 0)),
                  pl.BlockSpec((1, LANES), lambda i: (0, 0))],
        out_specs=pl.BlockSpec((1, 4 * N_HEADS, l), lambda i: (i, 0, 0)),
        out_shape=jax.ShapeDtypeStruct((b, 4 * N_HEADS, l), F32),
        compiler_params=_cparams("parallel"),
        name="dn_gates",
    )(ba, alog_lanes, dtb_lanes)


def _dn_masks():
    row = lax.broadcasted_iota(jnp.int32, (LANES, LANES), 0)
    col = lax.broadcasted_iota(jnp.int32, (LANES, LANES), 1)
    fwd = row < DN_CHUNK
    same = (row // DN_CHUNK) == (col // DN_CHUNK)
    tril = same & ((fwd & (row >= col)) | (~fwd & (row <= col)))
    strict = tril & (row != col)
    return row, col, tril, strict


def _dn_prepare(h, n_rows, raw_refs, conv_refs, bg_ref, pad_ref, qkv_s, gate_s, defer=False):
    gate_s[...] = jnp.zeros(gate_s.shape, F32)
    for d in range(2):
        gate_s[d, 0:1, :] = bg_ref[0, pl.ds(d * N_HEADS + h, 1), :]
        gate_s[d, 1:2, :] = bg_ref[0, pl.ds((2 + d) * N_HEADS + h, 1), :]
    n_tiles = n_rows // LANES
    zeros_halo = jnp.zeros((CONV_HALO, LANES), F32)
    for which in range(3):
        pad_ref[which, 0:CONV_HALO, :] = zeros_halo
        pad_ref[which, CONV_HALO + n_rows:2 * CONV_HALO + n_rows, :] = zeros_halo
        pad_ref[which, CONV_HALO:CONV_HALO + n_rows, :] = raw_refs[which][0, 0].astype(F32)
    taps = [conv_refs[which][0] for which in range(3)]

    def tile(t, carry=None):
        rows = pl.ds(pl.multiple_of(t * LANES, LANES), LANES)
        base = t * LANES + CONV_HALO - CONV_K // 2
        ys = []
        for which in range(3):
            acc = pad_ref[which, pl.ds(base, LANES), :] * taps[which][0:1, :]
            for j in range(1, CONV_K):
                acc = acc + pad_ref[which, pl.ds(base + j, LANES), :] * taps[which][j:j + 1, :]
            ys.append(_silu(acc))
        q, k, v = ys
        qkv_s[0, rows, :] = q * (lax.rsqrt(jnp.sum(q * q, axis=-1, keepdims=True) + EPS) * (HEAD_DIM ** -0.5))
        qkv_s[1, rows, :] = k * lax.rsqrt(jnp.sum(k * k, axis=-1, keepdims=True) + EPS)
        qkv_s[2, rows, :] = v
        return carry

    if defer:
        return [functools.partial(tile, t) for t in range(n_tiles)]
    lax.fori_loop(0, n_tiles, tile, 0, unroll=2)
    return []


def _mm(x, y):
    return jnp.dot(x.astype(BF16), y.astype(BF16), preferred_element_type=F32)


def _unit_tri_solve(a_all, rhs_all, row, col, tick):
    blk = lambda m: (row // m) == (col // m)
    eye = jnp.where(row == col, 1.0, 0.0)
    ad = [jnp.where(blk(DN_SOLVE_BASE), a, 0.0) for a in a_all]
    t = [eye - x for x in ad]
    p = [_mm(x, x) for x in ad]
    tick()
    for _ in range(int(math.log2(DN_SOLVE_BASE)) - 2):
        tp = [_mm(jnp.concatenate([ti, pi], axis=0), pi) for ti, pi in zip(t, p)]
        tick()
        t = [ti + x[:LANES] for ti, x in zip(t, tp)]
        p = [x[LANES:] for x in tp]
    tp = [_mm(ti, pi) for ti, pi in zip(t, p)]
    tick()
    t = [ti + x for ti, x in zip(t, tp)]
    k = 2 * DN_SOLVE_BASE
    while k < DN_CHUNK:
        nk = [_mm(ti, jnp.where(blk(k) & ~blk(k // 2), a, 0.0)) for ti, a in zip(t, a_all)]
        tick()
        tn = [_mm(x, ti) for x, ti in zip(nk, t)]
        tick()
        t = [ti - x for ti, x in zip(t, tn)]
        k *= 2
    ny = [_mm(ti, jnp.concatenate([jnp.where(blk(k // 2), 0.0, a), r], axis=1))
          for ti, a, r in zip(t, a_all, rhs_all)]
    tick()
    ny2 = [_mm(x[:, :LANES], x[:, LANES:]) for x in ny]
    tick()
    return [x[:, LANES:] - z for x, z in zip(ny, ny2)]


def _dn_intra(steps, n, qkv_s, gate_s, lhs_s, c_s, o0_s, eg_s, with_out, slot0=0, tick=lambda: None):
    row, col, tril, strict = _dn_masks()
    is_f = row < DN_CHUNK
    nt = (((1,), (1,)), ((), ()))

    def load(i):
        ri = pl.ds(pl.multiple_of(i * DN_CHUNK, DN_CHUNK), DN_CHUNK)
        rj = pl.ds(pl.multiple_of((n - 1 - i) * DN_CHUNK, DN_CHUNK), DN_CHUNK)
        pair = lambda ref, a, b: jnp.concatenate([ref[a, ri, :], ref[b, rj, :]], axis=0)
        q2, k2, v2 = pair(qkv_s, 0, 0), pair(qkv_s, 1, 1), pair(qkv_s, 2, 2)
        j = n - 1 - i
        ga = gate_s[0, :, pl.ds(pl.multiple_of((i // 2) * LANES, LANES), LANES)]
        gb = gate_s[1, :, pl.ds(pl.multiple_of((j // 2) * LANES, LANES), LANES)]
        lo = lax.broadcasted_iota(jnp.int32, ga.shape, 1) < DN_CHUNK
        gates = jnp.where(i % 2 == 0, jnp.where(lo, ga, gb),
                          pltpu.roll(jnp.where(lo, gb, ga), DN_CHUNK, axis=1))
        g_row = jnp.broadcast_to(gates[1:2, :], (LANES, LANES))
        g2 = g_row.T
        b2 = jnp.broadcast_to(gates[0:1, :], (LANES, LANES)).T
        decay = jnp.exp(jnp.where(tril, g2 - g_row, -jnp.inf))
        return dict(q2=q2, k2=k2, v2=v2, b2=b2, g2=g2, decay=decay, kb2=k2 * b2, eg=jnp.exp(g2))

    st = [load(i) for i in steps]
    if with_out:
        kk = [lax.dot_general(jnp.concatenate([d["kb2"], d["q2"]], axis=0).astype(BF16), d["k2"].astype(BF16), nt,
                              preferred_element_type=F32) for d in st]
        a_qk = [jnp.where(tril, x[LANES:] * d["decay"], 0.0) for x, d in zip(kk, st)]
    else:
        kk = [lax.dot_general(d["kb2"].astype(BF16), d["k2"].astype(BF16), nt, preferred_element_type=F32)
              for d in st]
    tick()
    a = [jnp.where(strict, x[:LANES] * d["decay"], 0.0) for x, d in zip(kk, st)]
    sol = _unit_tri_solve(a, [jnp.concatenate([d["v2"] * d["b2"], d["kb2"] * d["eg"]], axis=1) for d in st],
                          row, col, tick)
    split = lambda x: [jnp.where(is_f, x, 0.0), jnp.where(is_f, 0.0, x)]
    kwu = []
    for d, x in zip(st, sol):
        g2 = d["g2"]
        gl_f, gl_b = g2[DN_CHUNK - 1:DN_CHUNK, :], g2[DN_CHUNK:DN_CHUNK + 1, :]
        gl = jnp.concatenate([jnp.broadcast_to(gl_f, (DN_CHUNK, LANES)),
                              jnp.broadcast_to(gl_b, (DN_CHUNK, LANES))], axis=0)
        kd2 = d["k2"] * jnp.exp(gl - g2)
        d["egl"] = jnp.exp(jnp.concatenate([gl_f, gl_b, jnp.zeros((6, LANES), F32)], axis=0))
        kwu.append(_mm(kd2.T, jnp.concatenate(split(x[:, LANES:]) + split(x[:, :LANES]), axis=1)))
    if with_out:
        awu = [_mm(x, y) for x, y in zip(a_qk, sol)]
    tick()
    for k, i in enumerate(steps):
        i = slot0 + i
        c_s[i, 0] = kwu[k][:, 2 * LANES:3 * LANES]
        c_s[i, 1] = kwu[k][:, 3 * LANES:]
        eg_s[i] = st[k]["egl"]
        lhs_s[i, 0, 0:LANES, :] = (-kwu[k][:, :LANES]).astype(BF16)
        lhs_s[i, 1, 0:LANES, :] = (-kwu[k][:, LANES:2 * LANES]).astype(BF16)
        if with_out:
            q_eff = st[k]["q2"] * st[k]["eg"] - awu[k][:, LANES:]
            o0_s[i] = awu[k][:, :LANES]
            lhs_s[i, 0, LANES:LANES + DN_CHUNK, :] = q_eff[:DN_CHUNK].astype(BF16)
            lhs_s[i, 1, LANES:LANES + DN_CHUNK, :] = q_eff[DN_CHUNK:].astype(BF16)


def _dn_recur(i, n, s_f, s_b, lhs_s, c_s, o0_s, eg_s, o_s, with_out, slot0=0):
    m = LANES + DN_CHUNK if with_out else LANES
    k = slot0 + i
    r_f = jnp.dot(lhs_s[k, 0, 0:m, :], s_f.astype(BF16), preferred_element_type=F32)
    r_b = jnp.dot(lhs_s[k, 1, 0:m, :], s_b.astype(BF16), preferred_element_type=F32)
    if with_out:
        o0 = o0_s[i]
        o_s[0, pl.ds(pl.multiple_of(i * DN_CHUNK, DN_CHUNK), DN_CHUNK), :] = r_f[LANES:] + o0[:DN_CHUNK]
        o_s[1, pl.ds(pl.multiple_of((n - 1 - i) * DN_CHUNK, DN_CHUNK), DN_CHUNK), :] = r_b[LANES:] + o0[DN_CHUNK:]
    eg = eg_s[k]
    s_f = s_f * jnp.broadcast_to(eg[0:1, :], (LANES, LANES)) + r_f[:LANES] + c_s[k, 0]
    s_b = s_b * jnp.broadcast_to(eg[1:2, :], (LANES, LANES)) + r_b[:LANES] + c_s[k, 1]
    return s_f, s_b


def _delta_kernel(q_ref, k_ref, v_ref, z_ref, cq_ref, ck_ref, cv_ref, bg_ref, cbg_ref,
                  wq_ref, wk_ref, wv_ref, nw_ref, o_ref,
                  pad_s, qkv_s, gate_s, cqkv_s, cgate_s, lhs_s, c_s, o0_s, eg_s, o_s):
    h = pl.program_id(1)
    l = q_ref.shape[2]
    lc = cq_ref.shape[2]
    n, nc = l // DN_CHUNK, lc // DN_CHUNK
    conv_refs = (wq_ref, wk_ref, wv_ref)
    stage = (lhs_s, c_s, o0_s, eg_s)

    _dn_prepare(h, lc, (cq_ref, ck_ref, cv_ref), conv_refs, cbg_ref, pad_s, cqkv_s, cgate_s)
    lat_tiles = _dn_prepare(h, l, (q_ref, k_ref, v_ref), conv_refs, bg_ref, pad_s, qkv_s, gate_s, defer=True)

    group = math.gcd(DN_GROUP, n)
    n_groups = n // group
    gc = math.gcd(DN_GROUP, nc)

    def two_tiles():
        for _ in range(2):
            if lat_tiles:
                lat_tiles.pop(0)()

    for g in range(nc // gc):
        _dn_intra([g * gc + k for k in range(gc)], nc, cqkv_s, cgate_s, *stage, with_out=False, slot0=n,
                  tick=two_tiles)
    while lat_tiles:
        two_tiles()
    state = [jnp.zeros((LANES, LANES), F32)] * 2

    def ctx_step(i):
        state[:] = _dn_recur(i, nc, state[0], state[1], *stage, o_s, with_out=False, slot0=n)

    def lat_step(i):
        state[:] = _dn_recur(i, n, state[0], state[1], *stage, o_s, with_out=True)

    def lat_group(g, scan_steps):
        pending = list(scan_steps)
        tick = lambda: pending.pop(0)() if pending else None
        _dn_intra([g * group + k for k in range(group)], n, qkv_s, gate_s, *stage, with_out=True, tick=tick)
        while pending:
            tick()

    lat_group(0, [functools.partial(ctx_step, i) for i in range(nc)])

    def pipelined(g, s):
        state[:] = s
        lat_group(g, [functools.partial(lat_step, (g - 1) * group + k) for k in range(group)])
        return tuple(state)

    state[:] = lax.fori_loop(1, n_groups, pipelined, tuple(state))

    nw = nw_ref[...]

    def out_tile(t):
        rows = pl.ds(t * LANES, LANES)
        o = o_s[0, rows, :] + o_s[1, rows, :]
        y = o * lax.rsqrt(jnp.mean(o * o, axis=-1, keepdims=True) + EPS) * nw
        o_ref[0, 0, rows, :] = (y * _silu(z_ref[0, 0, rows, :].astype(F32))).astype(o_ref.dtype)

    per_tile = LANES // DN_CHUNK
    final_at = lambda t: max(per_tile * t + per_tile - 1, n - 1 - per_tile * t)
    first = (n_groups - 1) * group
    ready = [t for t in range(l // LANES) if final_at(t) < first]
    for i in range(first, n):
        lat_step(i)
        ready += [t for t in range(l // LANES) if final_at(t) == i]
        if ready:
            out_tile(ready.pop(0))
    for t in ready:
        out_tile(t)


def _delta_branch(p_lat, p_ctx, bg, cbg, conv_taps, norm_w):
    b, _, l, _ = p_lat.shape
    lc = p_ctx.shape[2]
    n, nc = l // DN_CHUNK, lc // DN_CHUNK
    hb =lambda off: pl.BlockSpec((1, 1, l, LANES), lambda i, j, off=off: (i, off + j, 0, 0))
    cb = lambda off: pl.BlockSpec((1, 1, lc, LANES), lambda i, j, off=off: (i, off + j, 0, 0))
    tb = lambda off: pl.BlockSpec((1, 8, LANES), lambda i, j, off=off: (off + j, 0, 0))
    return pl.pallas_call(
        _delta_kernel,
        grid=(b, N_HEADS),
        in_specs=[hb(0), hb(N_HEADS), hb(2 * N_HEADS), hb(3 * N_HEADS),
                  cb(0), cb(N_HEADS), cb(2 * N_HEADS),
                  pl.BlockSpec((1, 4 * N_HEADS, l), lambda i, j: (i, 0, 0)),
                  pl.BlockSpec((1, 4 * N_HEADS, lc), lambda i, j: (i, 0, 0)),
                  tb(0), tb(N_HEADS), tb(2 * N_HEADS),
                  pl.BlockSpec((1, LANES), lambda i, j: (0, 0))],
        out_specs=pl.BlockSpec((1, 1, l, LANES), lambda i, j: (i, j, 0, 0)),
        out_shape=jax.ShapeDtypeStruct((b, N_HEADS, l, LANES), BF16),
        scratch_shapes=[pltpu.VMEM((3, l + 2 * CONV_HALO, LANES), F32),
                        pltpu.VMEM((3, l, LANES), F32),
                        pltpu.VMEM((2, 8, l), F32),
                        pltpu.VMEM((3, lc, LANES), F32),
                        pltpu.VMEM((2, 8, lc), F32),
                        pltpu.VMEM((n + nc, 2, LANES + DN_CHUNK, LANES), BF16),
                        pltpu.VMEM((n + nc, 2, LANES, LANES), F32),
                        pltpu.VMEM((n, LANES, LANES), F32),
                        pltpu.VMEM((n + nc, 8, LANES), F32),
                        pltpu.VMEM((2, l, LANES), F32)],
        compiler_params=_cparams("parallel", "arbitrary"),
        name="delta_branch",
    )(p_lat, p_lat, p_lat, p_lat, p_ctx, p_ctx, p_ctx, bg, cbg,
      conv_taps, conv_taps, conv_taps, norm_w.reshape(1, LANES))


def _rope_perm():
    quarter = HEAD_DIM // 4
    return np.concatenate([np.arange(quarter), 2 * quarter + np.arange(quarter),
                           quarter + np.arange(quarter), 3 * quarter + np.arange(quarter)])


def _rope(x, cos, sin):
    return x * cos + pltpu.roll(x, HEAD_DIM // 2, axis=1) * sin


def _head_rms(x, w):
    return x * lax.rsqrt(jnp.mean(x * x, axis=-1, keepdims=True) + EPS) * w


def _na_kernel(q_ref, k_ref, v_ref, ck_ref, cv_ref, qnw_ref, knw_ref, cos_ref, sin_ref, bias_ref, o_ref,
               q_s, k_s, ck_s):
    l = q_ref.shape[2]
    rows = l // GRID_W
    kh = min(NA_KH, rows)
    n_loc = kh * GRID_W
    qnw, knw = qnw_ref[...], knw_ref[...]

    def prep(t):
        r = pl.ds(t * LANES, LANES)
        cos, sin = cos_ref[r, :], sin_ref[r, :]
        q = _rope(_head_rms(q_ref[0, 0, r, :].astype(F32), qnw), cos, sin) * (HEAD_DIM ** -0.5)
        q_s[r, :] = q.astype(BF16)
        k_s[r, :] = _rope(_head_rms(k_ref[0, 0, r, :].astype(F32), knw), cos, sin).astype(BF16)

    ck_s[...] = _head_rms(ck_ref[0, 0].astype(F32), knw).astype(BF16)
    nt = (((1,), (1,)), ((), ()))
    group = math.gcd(NA_ROW_GROUP, rows)
    window_start = lambda r: min(max(r - kh // 2, 0), rows - kh)
    tiles_needed = lambda g: -(-(window_start(g * group + group - 1) + kh) * GRID_W // LANES)

    def row_group(g, pending):
        tick = lambda: pending.pop(0)() if pending else None
        rr = [g * group + k for k in range(group)]
        rs = [window_start(r) for r in rr]
        qr = [q_s[pl.ds(r * GRID_W, GRID_W), :] for r in rr]
        kloc = [pl.ds(x * GRID_W, n_loc) for x in rs]
        s_loc = [lax.dot_general(q, k_s[kl, :], nt, preferred_element_type=F32) for q, kl in zip(qr, kloc)]
        tick()
        s_ctx = [lax.dot_general(q, ck_s[...], nt, preferred_element_type=F32) for q in qr]
        tick()
        s_loc = [s + bias_ref[0, r - x] for s, r, x in zip(s_loc, rr, rs)]
        m = [jnp.maximum(jnp.max(sl, axis=-1, keepdims=True), jnp.max(sc, axis=-1, keepdims=True))
             for sl, sc in zip(s_loc, s_ctx)]
        tick()
        p_loc = [jnp.exp(s - mi) for s, mi in zip(s_loc, m)]
        p_ctx = [jnp.exp(s - mi) for s, mi in zip(s_ctx, m)]
        tick()
        denom = [jnp.sum(pl_, axis=-1, keepdims=True) + jnp.sum(pc, axis=-1, keepdims=True)
                 for pl_, pc in zip(p_loc, p_ctx)]
        o_loc = [jnp.dot(p.astype(BF16), v_ref[0, 0, kl, :], preferred_element_type=F32)
                 for p, kl in zip(p_loc, kloc)]
        o_ctx = [jnp.dot(p.astype(BF16), cv_ref[0, 0], preferred_element_type=F32) for p in p_ctx]
        while pending:
            tick()
        for r, ol, oc, d in zip(rr, o_loc, o_ctx, denom):
            o_ref[0, 0, pl.ds(r * GRID_W, GRID_W), :] = ((ol + oc) / d).astype(o_ref.dtype)

    n_groups = rows // group
    for t in range(tiles_needed(0)):
        prep(t)
    for g in range(n_groups):
        nxt = range(tiles_needed(g), tiles_needed(g + 1)) if g + 1 < n_groups else ()
        row_group(g, [functools.partial(prep, t) for t in nxt])


def _na_tables(l):
    pos = jnp.arange(l)
    row = (pos // GRID_W).astype(F32)
    col = (pos % GRID_W).astype(F32)
    half = HEAD_DIM // 2
    inv_freq = ROPE_THETA ** (-jnp.arange(0, half, 2, dtype=F32) / half)
    ang_r = row[:, None] * inv_freq[None, :]
    ang_c = col[:, None] * inv_freq[None, :]
    cos = jnp.concatenate([jnp.cos(ang_r), jnp.cos(ang_r), jnp.cos(ang_c), jnp.cos(ang_c)], axis=-1)
    sin = jnp.concatenate([-jnp.sin(ang_r), jnp.sin(ang_r), -jnp.sin(ang_c), jnp.sin(ang_c)], axis=-1)
    perm = _rope_perm()
    return cos[:, perm], sin[:, perm]


def _na_bias_table(rpb, rows):
    kh = min(NA_KH, rows)
    t = np.arange(NA_KH)[:, None]
    i = np.arange(kh)[None, :]
    dr = np.clip(i - t + NA_KH - 1, 0, 2 * NA_KH - 2)
    q = np.arange(GRID_W)[:, None]
    kc = np.arange(GRID_W)[None, :]
    qstart = np.clip(q - NA_KW // 2, 0, GRID_W - NA_KW)
    in_win = (kc >= qstart) & (kc < qstart + NA_KW)
    dc = np.clip(kc - q + NA_KW - 1, 0, 2 * NA_KW - 2)
    pick_r = (dr[:, :, None] == np.arange(2 * NA_KH - 1)).astype(np.float32)
    pick_c = (np.arange(2 * NA_KW - 1)[:, None, None] == dc[None]).astype(np.float32)
    tab = jnp.einsum("tir,hrc,cqk->htqik", pick_r, rpb.astype(F32), pick_c, precision=lax.Precision.HIGHEST)
    tab = jnp.where(jnp.asarray(in_win)[None, None, :, None, :], tab, NEG_INF)
    return tab.reshape(rpb.shape[0], NA_KH, GRID_W, kh * GRID_W)


def _na_branch(p_lat, p_ctx, q_norm_w, k_norm_w, rpb):
    b, _, l, _ = p_lat.shape
    lc = p_ctx.shape[2]
    rows = l // GRID_W
    n_loc = min(NA_KH, rows) * GRID_W
    cos, sin = _na_tables(l)
    bias = _na_bias_table(rpb, rows)
    hb = lambda off: pl.BlockSpec((1, 1, l, LANES), lambda j, i, off=off: (i, off + j, 0, 0))
    cb = lambda off: pl.BlockSpec((1, 1, lc, LANES), lambda j, i, off=off: (i, off + j, 0, 0))
    const = lambda shape: pl.BlockSpec(shape, lambda j, i: (0,) * len(shape))
    return pl.pallas_call(
        _na_kernel,
        grid=(N_HEADS, b),
        in_specs=[hb(4 * N_HEADS), hb(5 * N_HEADS), hb(6 * N_HEADS), cb(3 * N_HEADS), cb(4 * N_HEADS),
                  const((1, LANES)), const((1, LANES)), const((l, LANES)), const((l, LANES)),
                  pl.BlockSpec((1, NA_KH, GRID_W, n_loc), lambda j, i: (j, 0, 0, 0))],
        out_specs=pl.BlockSpec((1, 1, l, LANES), lambda j, i: (i, j, 0, 0)),
        out_shape=jax.ShapeDtypeStruct((b, N_HEADS, l, LANES), BF16),
        scratch_shapes=[pltpu.VMEM((l, LANES), BF16), pltpu.VMEM((l, LANES), BF16),
                        pltpu.VMEM((lc, LANES), BF16)],
        compiler_params=_cparams("parallel", "arbitrary"),
        name="na_branch",
    )(p_lat, p_lat, p_lat, p_ctx, p_ctx, q_norm_w[_rope_perm()].reshape(1, LANES),
      k_norm_w[_rope_perm()].reshape(1, LANES), cos, sin, bias)


def _merge_kernel(a_ref, b_ref, wa_ref, wb_ref, ga_ref, gb_ref, o_ref, a_s, b_s):
    @pl.when(pl.program_id(1) == 0)
    def _():
        for k in range(N_HEADS):
            a_s[:, k * LANES:(k + 1) * LANES] = a_ref[0, k]
            b_s[:, k * LANES:(k + 1) * LANES] = b_ref[0, k]

    ya = jnp.dot(a_s[...], wa_ref[...], preferred_element_type=F32)
    yb = jnp.dot(b_s[...], wb_ref[...], preferred_element_type=F32)
    for k in range(ga_ref.shape[1]):
        cols = slice(k * LANES, (k + 1) * LANES)
        o_ref[:, cols] = (_sigmoid(ga_ref[0, k].astype(F32)) * ya[:, cols]
                          + _sigmoid(gb_ref[0, k].astype(F32)) * yb[:, cols]).astype(o_ref.dtype)


def _merge(dn_o, na_o, w_a, w_b, p_lat, tm=1024, tn=512):
    b, _, l, _ = dn_o.shape
    d = w_a.shape[1]
    tpb = l // tm
    nb = tn // LANES
    head_blk = pl.BlockSpec((1, N_HEADS, tm, LANES), lambda i, j: (i // tpb, 0, i % tpb, 0))
    gate_blk = lambda off: pl.BlockSpec((1, nb, tm, LANES),
                                        lambda i, j, off=off: (i // tpb, off // nb + j, i % tpb, 0))
    w_blk = pl.BlockSpec((N_HEADS * LANES, tn), lambda i, j: (0, j))
    return pl.pallas_call(
        _merge_kernel,
        grid=(b * tpb, d // tn),
        in_specs=[head_blk, head_blk, w_blk, w_blk, gate_blk(7 * N_HEADS), gate_blk(8 * N_HEADS)],
        out_specs=pl.BlockSpec((tm, tn), lambda i, j: (i, j)),
        out_shape=jax.ShapeDtypeStruct((b * l, d), BF16),
        scratch_shapes=[pltpu.VMEM((tm, N_HEADS * LANES), BF16), pltpu.VMEM((tm, N_HEADS * LANES), BF16)],
        compiler_params=_cparams("parallel", "arbitrary"),
        name="merge",
    )(dn_o, na_o, w_a, w_b, p_lat, p_lat)


def _out_kernel(y_ref, w_ref, x_ref, g1_ref, nw_ref, sc_ref, sh_ref, wr_hi_ref, wr_lo_ref, x1_ref, h2_ref, lg_ref):
    x1 = x_ref[...] + g1_ref[0] * jnp.dot(y_ref[...], w_ref[...], preferred_element_type=F32)
    x1_ref[...] = x1
    h2 = (x1 * lax.rsqrt(jnp.mean(x1 * x1, axis=-1, keepdims=True) + EPS) * nw_ref[...]
          * (1.0 + sc_ref[0]) + sh_ref[0])
    h_hi = h2.astype(BF16)
    h2_ref[...] = h_hi
    h_lo = (h2 - h_hi.astype(F32)).astype(BF16)
    lg_ref[...] = (jnp.dot(h_hi, wr_hi_ref[...], preferred_element_type=F32)
                   + (jnp.dot(h_hi, wr_lo_ref[...], preferred_element_type=F32)
                      + jnp.dot(h_lo, wr_hi_ref[...], preferred_element_type=F32)))


def _out_proj(y, w_out, x2d, g1, norm_w, scale, shift, w_router_pad, l, tm=256):
    m, d = x2d.shape
    tpb = l // tm
    wr_hi = w_router_pad.astype(BF16)
    wr_lo = (w_router_pad - wr_hi.astype(F32)).astype(BF16)
    row_blk = lambda: pl.BlockSpec((tm, d), lambda i: (i, 0))
    mod_blk = lambda: pl.BlockSpec((1, 1, d), lambda i: (i // tpb, 0, 0))
    return pl.pallas_call(
        _out_kernel,
        grid=(m // tm,),
        in_specs=[row_blk(), pl.BlockSpec((d, d), lambda i: (0, 0)), row_blk(), mod_blk(),
                  pl.BlockSpec((1, d), lambda i: (0, 0)), mod_blk(), mod_blk(),
                  pl.BlockSpec((d, LANES), lambda i: (0, 0)), pl.BlockSpec((d, LANES), lambda i: (0, 0))],
        out_specs=[row_blk(), row_blk(), pl.BlockSpec((tm, LANES), lambda i: (i, 0))],
        out_shape=[jax.ShapeDtypeStruct((m, d), F32), jax.ShapeDtypeStruct((m, d), BF16),
                   jax.ShapeDtypeStruct((m, LANES), F32)],
        compiler_params=_cparams("parallel"),
        name="out_proj",
    )(y, w_out, x2d, g1, norm_w.reshape(1, d), scale, shift, wr_hi, wr_lo)


def _route_kernel(lg_ref, slot_t_ref, gate_t_ref, slot_ref, aff_s, slot_s):
    l = lg_ref.shape[0]
    n_tiles = l // LANES
    cap = EC_CAPACITY_FACTOR * l // N_EXPERTS
    lane = lax.broadcasted_iota(jnp.int32, (LANES, LANES), 1)
    row = lax.broadcasted_iota(jnp.int32, (LANES, LANES), 0)

    for t in range(n_tiles):
        x = jnp.where(lane < N_EXPERTS, lg_ref[t * LANES:(t + 1) * LANES, :], -jnp.inf)
        e = jnp.exp(x - jnp.max(x, axis=-1, keepdims=True))
        aff = e / jnp.sum(e, axis=-1, keepdims=True)
        aff_s[:, t * LANES:(t + 1) * LANES] = aff.T[:N_EXPERTS]

    aff_t = aff_s[...]
    keys = pltpu.bitcast(aff_t, jnp.int32)
    count_ge = lambda thr: jnp.sum(jnp.where(keys >= thr, 1.0, 0.0), axis=1, keepdims=True)

    def bisect(_, c):
        lo, hi = c
        mid = lo + (hi - lo) // 2
        ok = count_ge(mid) >= cap
        return jnp.where(ok, mid, lo), jnp.where(ok, hi, mid)

    inf_bits = 0x7F800000
    thr, _ = lax.fori_loop(0, 31, bisect, (jnp.zeros((N_EXPERTS, 1), jnp.int32),
                                           jnp.full((N_EXPERTS, 1), inf_bits, jnp.int32)))
    gt, eq = keys > thr, keys == thr
    need = cap - jnp.sum(jnp.where(gt, 1.0, 0.0), axis=1, keepdims=True)
    before = jnp.where(row < lane, 1.0, 0.0).astype(BF16)

    def excl_prefix(flags):
        out, off = [], jnp.zeros((N_EXPERTS, 1), F32)
        ones = jnp.where(flags, 1.0, 0.0)
        for t in range(n_tiles):
            f = ones[:, t * LANES:(t + 1) * LANES]
            out.append(jnp.dot(f.astype(BF16), before, preferred_element_type=F32) + off)
            off = off + jnp.sum(f, axis=1, keepdims=True)
        return jnp.concatenate(out, axis=1)

    sel = gt | (eq & (excl_prefix(eq) < need))
    slot_f = jnp.where(sel, excl_prefix(sel), -1.0)
    slot_t_ref[0] = slot_f.astype(jnp.int32)
    gate_t_ref[0] = aff_t
    slot_s[...] = jnp.full(slot_s.shape, -1.0, F32)
    slot_s[0:N_EXPERTS, :] = slot_f
    for t in range(n_tiles):
        slot_ref[0, t * LANES:(t + 1) * LANES, :] = slot_s[:, t * LANES:(t + 1) * LANES].T.astype(jnp.int32)


def _route(logits, b, l):
    return pl.pallas_call(
        _route_kernel,
        grid=(b,),
        in_specs=[pl.BlockSpec((l, LANES), lambda i: (i, 0))],
        out_specs=[pl.BlockSpec((1, N_EXPERTS, l), lambda i: (i, 0, 0)),
                   pl.BlockSpec((1, N_EXPERTS, l), lambda i: (i, 0, 0)),
                   pl.BlockSpec((1, l, LANES), lambda i: (i, 0, 0))],
        out_shape=[jax.ShapeDtypeStruct((b, N_EXPERTS, l), jnp.int32),
                   jax.ShapeDtypeStruct((b, N_EXPERTS, l), F32),
                   jax.ShapeDtypeStruct((b, l, LANES), jnp.int32)],
        scratch_shapes=[pltpu.VMEM((N_EXPERTS, l), F32), pltpu.VMEM((LANES, l), F32)],
        compiler_params=_cparams("parallel"),
        name="route",
    )(logits)


def _ffn_kernel(h_ref, slot_t_ref, gate_t_ref, w1_ref, w3_ref, w2_ref, o_ref):
    e = pl.program_id(0)
    cap, l = o_ref.shape[2], h_ref.shape[1]
    hit = lax.broadcasted_iota(jnp.int32, (cap, l), 0) == slot_t_ref[0, pl.ds(e, 1), :]
    gate = jnp.sum(jnp.where(hit, gate_t_ref[0, pl.ds(e, 1), :], 0.0), axis=1, keepdims=True)
    x = jnp.dot(jnp.where(hit, 1.0, 0.0).astype(BF16), h_ref[0], preferred_element_type=F32).astype(BF16)
    h1 = jnp.dot(x, w1_ref[0], preferred_element_type=F32)
    h3 = jnp.dot(x, w3_ref[0], preferred_element_type=F32)
    hid = (_silu(h1) * h3).astype(BF16)
    o_ref[0, 0] = (jnp.dot(hid, w2_ref[0], preferred_element_type=F32) * gate).astype(o_ref.dtype)


def _expert_ffn(h2, slot_t, gate_t, w1, w3, w2, cap):
    b, l, d = h2.shape
    e, _, f = w1.shape
    return pl.pallas_call(
        _ffn_kernel,
        grid=(e, b),
        in_specs=[pl.BlockSpec((1, l, d), lambda j, i: (i, 0, 0)),
                  pl.BlockSpec((1, e, l), lambda j, i: (i, 0, 0)),
                  pl.BlockSpec((1, e, l), lambda j, i: (i, 0, 0)),
                  pl.BlockSpec((1, d, f), lambda j, i: (j, 0, 0)),
                  pl.BlockSpec((1, d, f), lambda j, i: (j, 0, 0)),
                  pl.BlockSpec((1, f, d), lambda j, i: (j, 0, 0))],
        out_specs=pl.BlockSpec((1, 1, cap, d), lambda j, i: (i, j, 0, 0)),
        out_shape=jax.ShapeDtypeStruct((b, e, cap, d), BF16),
        compiler_params=_cparams("parallel", "arbitrary"),
        name="expert_ffn",
    )(h2, slot_t, gate_t, w1, w3, w2)


def _combine_kernel(slot_ref, y_ref, x1_ref, g2_ref, o_ref):
    tm = slot_ref.shape[1]
    cap = y_ref.shape[1] // N_EXPERTS
    slot = slot_ref[0]
    j = lax.broadcasted_iota(jnp.int32, (tm, cap), 1)
    onehot = jnp.concatenate([jnp.where(slot[:, e:e + 1] == j, 1.0, 0.0).astype(BF16) for e in range(N_EXPERTS)],
                             axis=1)
    o_ref[0] = x1_ref[0] + g2_ref[0] * jnp.dot(onehot, y_ref[0], preferred_element_type=F32)


def _combine(slot, ye, x1, g2, tm=512, tn=1024):
    b, l, d = x1.shape
    ec = ye.shape[1]
    return pl.pallas_call(
        _combine_kernel,
        grid=(b, d // tn, l // tm),
        in_specs=[pl.BlockSpec((1, tm, LANES), lambda i, n, m: (i, m, 0)),
                  pl.BlockSpec((1, ec, tn), lambda i, n, m: (i, 0, n)),
                  pl.BlockSpec((1, tm, tn), lambda i, n, m: (i, m, n)),
                  pl.BlockSpec((1, 1, tn), lambda i, n, m: (i, 0, n))],
        out_specs=pl.BlockSpec((1, tm, tn), lambda i, n, m: (i, m, n)),
        out_shape=jax.ShapeDtypeStruct((b, l, d), F32),
        compiler_params=_cparams("parallel", "parallel", "arbitrary"),
        name="moe_combine",
    )(slot, ye, x1, g2)


def _layer(x, ctx, mod, mod_c, norm1_w, w_in, conv_w, a_log, dt_bias, dn_norm_w, q_norm_w, k_norm_w, rpb,
           w_a, w_b, w_out, norm2_w, w_router, w1, w3, w2):
    b, l, d = x.shape
    sh1, sc1, g1, sh2, sc2, g2 = [m[:, None, :] for m in jnp.split(mod, 6, axis=-1)]
    sh1c, sc1c = mod_c[None, None, :d], mod_c[None, None, d:2 * d]

    hd = N_HEADS * HEAD_DIM
    w_pack = _pack_w_in(w_in, 4 * hd, 4 * N_HEADS, 4 * hd, 6 * hd)
    n_main = 7 * hd + 2 * d
    tn = 1024
    p_lat, ba = _norm_proj(x, norm1_w, sc1, sh1, w_pack, 1024, tn, n_main // tn, lambda j: j, n_main)
    skip_from, skip = 3 * hd // tn, 2 * hd // tn
    p_ctx, cba = _norm_proj(ctx, norm1_w, sc1c, sh1c, w_pack, min(1024, b * ctx.shape[1]), tn, 5 * hd // tn,
                            lambda j: jnp.where(j < skip_from, j, j + skip), n_main)

    lanes_pad = lambda v: jnp.pad(v.reshape(1, -1).astype(F32), ((0, 0), (2 * N_HEADS, LANES - 4 * N_HEADS)))
    alog_l, dtb_l = lanes_pad(a_log), lanes_pad(dt_bias)
    bg, cbg = _dn_gates(ba, alog_l, dtb_l), _dn_gates(cba, alog_l, dtb_l)
    taps = jnp.pad(conv_w.astype(F32), ((0, 8 - CONV_K), (0, 0))).reshape(8, 3 * N_HEADS, LANES).transpose(1, 0, 2)
    dn_o = _delta_branch(p_lat, p_ctx, bg, cbg, taps, dn_norm_w.astype(F32))
    na_o = _na_branch(p_lat, p_ctx, q_norm_w.astype(F32), k_norm_w.astype(F32), rpb)

    y = _merge(dn_o, na_o, w_a.astype(BF16), w_b.astype(BF16), p_lat)
    w_router_pad = jnp.pad(w_router.astype(F32), ((0, 0), (0, LANES - N_EXPERTS)))
    x1, h2, logits = _out_proj(y, w_out.astype(BF16), x.reshape(b * l, d), g1, norm2_w, sc2, sh2,
                               w_router_pad, l)

    cap = EC_CAPACITY_FACTOR * l // N_EXPERTS
    slot_t, gate_t, slot = _route(logits, b, l)
    ye = _expert_ffn(h2.reshape(b, l, d), slot_t, gate_t, w1.astype(BF16), w3.astype(BF16), w2.astype(BF16), cap)
    return _combine(slot, ye.reshape(b, N_EXPERTS * cap, d), x1.reshape(b, l, d), g2)


def kernel(x, c, ctx, c_ctx, ada_w, ada_b, norm1_w, w_in, conv_w, dn_a_log, dn_dt_bias, dn_norm_w,
           na_q_norm_w, na_k_norm_w, na_rpb, w_branch_a, w_branch_b, w_out, norm2_w, w_router,
           expert_w1, expert_w3, expert_w2):
    b = x.shape[0]
    depth = ada_w.shape[0]
    cvec = jnp.concatenate([c, c_ctx[None, :], jnp.zeros((16 - b - 1, c.shape[1]), c.dtype)], axis=0)
    for i in range(depth):
        mod_all = _adaln_mod(cvec, ada_w[i], ada_b[i])
        x = _layer(x, ctx, mod_all[:b], mod_all[b], norm1_w[i], w_in[i], conv_w[i], dn_a_log[i],
                   dn_dt_bias[i], dn_norm_w[i], na_q_norm_w[i], na_k_norm_w[i], na_rpb[i],
                   w_branch_a[i], w_branch_b[i], w_out[i], norm2_w[i], w_router[i],
                   expert_w1[i], expert_w3[i], expert_w2[i])
    return x
```

```python
import functools
import math

import numpy as np
import jax
import jax.numpy as jnp
from jax import lax
from jax.experimental import pallas as pl
from jax.experimental.pallas import tpu as pltpu

F32 = jnp.float32
BF16 = jnp.bfloat16

EPS = 1e-6
NEG_INF = -1e30
LANES = 128
GRID_W = 64
N_HEADS = 16
HEAD_DIM = 128
CONV_K = 5
CONV_HALO = 16
DN_CHUNK = 64
DN_GROUP = 8
DN_SOLVE_BASE = 16
NA_KH = 8
NA_KW = 16
NA_ROW_GROUP = 8
ROPE_THETA = 10000.0
N_EXPERTS = 16
EC_CAPACITY_FACTOR = 2
VMEM_LIMIT = 56 * 1024 * 1024


def _cparams(*sem):
    return pltpu.CompilerParams(dimension_semantics=sem, vmem_limit_bytes=VMEM_LIMIT)


def _sigmoid(x):
    return 1.0 / (1.0 + jnp.exp(-x))


def _silu(x):
    return x * _sigmoid(x)


def _mod_kernel(c_ref, w_ref, b_ref, o_ref):
    s, w = _silu(c_ref[...]), w_ref[...]
    s_hi, w_hi = s.astype(BF16), w.astype(BF16)
    s_lo, w_lo = (s - s_hi.astype(F32)).astype(BF16), (w - w_hi.astype(F32)).astype(BF16)
    dot = lambda a, b: jnp.dot(a, b, preferred_element_type=F32)
    o_ref[...] = dot(s_hi, w_hi) + (dot(s_hi, w_lo) + dot(s_lo, w_hi)) + b_ref[...]


def _adaln_mod(cvec, ada_w, ada_b, tn=1024):
    m, d = cvec.shape
    n = ada_w.shape[1]
    return pl.pallas_call(
        _mod_kernel,
        grid=(n // tn,),
        in_specs=[pl.BlockSpec((m, d), lambda j: (0, 0)),
                  pl.BlockSpec((d, tn), lambda j: (0, j)),
                  pl.BlockSpec((1, tn), lambda j: (0, j))],
        out_specs=pl.BlockSpec((m, tn), lambda j: (0, j)),
        out_shape=jax.ShapeDtypeStruct((m, n), F32),
        compiler_params=_cparams("parallel"),
        name="adaln_mod",
    )(cvec, ada_w, ada_b.reshape(1, n))


PACK_BLK = 2 * LANES


def _pack_kernel(a_ref, b_ref, o_ref, *, gap_blk, n_main, rot_end, gap_w):
    j = pl.program_id(0)
    shifted = lambda: jnp.concatenate([a_ref[:, gap_w:], b_ref[:, :gap_w]], axis=1)

    def relabel(t):
        q = HEAD_DIM // 4
        heads = [t[:, k * HEAD_DIM:(k + 1) * HEAD_DIM] for k in range(PACK_BLK // HEAD_DIM)]
        return jnp.concatenate([jnp.concatenate([hx[:, 0:q], hx[:, 2 * q:3 * q], hx[:, q:2 * q], hx[:, 3 * q:]], axis=1)
                                for hx in heads], axis=1)

    @pl.when(j < gap_blk)
    def _():
        o_ref[...] = a_ref[...].astype(o_ref.dtype)

    @pl.when((j >= gap_blk) & (j < rot_end))
    def _():
        o_ref[...] = relabel(shifted()).astype(o_ref.dtype)

    @pl.when((j >= rot_end) & (j < n_main))
    def _():
        o_ref[...] = shifted().astype(o_ref.dtype)

    @pl.when(j == n_main)
    def _():
        o_ref[...] = jnp.concatenate([a_ref[:, :gap_w], jnp.zeros((a_ref.shape[0], PACK_BLK - gap_w), F32)],
                                     axis=1).astype(o_ref.dtype)


def _pack_w_in(w_in, gap_start, gap_w, rot_start, rot_stop):
    d, n = w_in.shape
    assert gap_start % PACK_BLK == 0 and (n - gap_w) % PACK_BLK == 0 and rot_start == gap_start
    gap_blk, n_main = gap_start // PACK_BLK, (n - gap_w) // PACK_BLK
    kern = functools.partial(_pack_kernel, gap_blk=gap_blk, n_main=n_main, rot_end=rot_stop // PACK_BLK, gap_w=gap_w)
    return pl.pallas_call(
        kern,
        grid=(n_main + 1,),
        in_specs=[pl.BlockSpec((d, PACK_BLK), lambda j: (0, jnp.where(j == n_main, gap_blk, j))),
                  pl.BlockSpec((d, PACK_BLK), lambda j: (0, jnp.minimum(j + 1, n_main)))],
        out_specs=pl.BlockSpec((d, PACK_BLK), lambda j: (0, j)),
        out_shape=jax.ShapeDtypeStruct((d, (n_main + 1) * PACK_BLK), BF16),
        compiler_params=_cparams("parallel"),
        name="pack_w_in",
    )(w_in, w_in)


def _norm_proj_kernel(x_ref, nw_ref, sc_ref, sh_ref, w_ref, wg_ref, o_ref, og_ref, h_ref):
    n_seq, _, rows, _ = o_ref.shape

    @pl.when(pl.program_id(1) == 0)
    def _():
        x = x_ref[...]
        y = x * lax.rsqrt(jnp.mean(x * x, axis=-1, keepdims=True) + EPS) * nw_ref[...]
        h_ref[...] = (y * (1.0 + sc_ref[0]) + sh_ref[0]).astype(BF16)
        og = jnp.dot(h_ref[...], wg_ref[...], preferred_element_type=F32)
        for s in range(n_seq):
            og_ref[s] = og[s * rows:(s + 1) * rows]

    acc = jnp.dot(h_ref[...], w_ref[...], preferred_element_type=F32)
    for s in range(n_seq):
        for k in range(o_ref.shape[1]):
            o_ref[s, k] = acc[s * rows:(s + 1) * rows, k * LANES:(k + 1) * LANES].astype(o_ref.dtype)


def _norm_proj(x, norm_w, scale, shift, w, tm, tn, n_tiles, col_tile, gates_col):
    b, l, d = x.shape
    n = n_tiles * tn
    tpb = max(l // tm, 1)
    spt = max(tm // l, 1)
    per_sample = scale.shape[0] != 1
    assert spt == 1 or not per_sample
    mod_idx = (lambda i, j: (i // tpb, 0, 0)) if per_sample else (lambda i, j: (0, 0, 0))
    return pl.pallas_call(
        _norm_proj_kernel,
        grid=(b * l // tm, n // tn),
        in_specs=[pl.BlockSpec((tm, d), lambda i, j: (i, 0)),
                  pl.BlockSpec((1, d), lambda i, j: (0, 0)),
                  pl.BlockSpec((1, 1, d), mod_idx),
                  pl.BlockSpec((1, 1, d), mod_idx),
                  pl.BlockSpec((d, tn), lambda i, j: (0, col_tile(j))),
                  pl.BlockSpec((d, LANES), lambda i, j: (0, gates_col // LANES))],
        out_specs=[pl.BlockSpec((spt, tn // LANES, tm // spt, LANES), lambda i, j: (i // tpb, j, i % tpb, 0)),
                   pl.BlockSpec((spt, tm // spt, LANES), lambda i, j: (i // tpb, i % tpb, 0))],
        out_shape=[jax.ShapeDtypeStruct((b, n // LANES, l, LANES), BF16),
                   jax.ShapeDtypeStruct((b, l, LANES), F32)],
        scratch_shapes=[pltpu.VMEM((tm, d), BF16)],
        compiler_params=_cparams("parallel", "arbitrary"),
        name="norm_proj",
    )(x.reshape(b * l, d), norm_w.reshape(1, d), scale, shift, w, w)


def _dn_gates_kernel(x_ref, alog_ref, dtb_ref, o_ref):
    l = x_ref.shape[1]
    lane = lax.broadcasted_iota(jnp.int32, (LANES, LANES), 1)
    row = lax.broadcasted_iota(jnp.int32, (LANES, LANES), 0)
    same_chunk = (row // DN_CHUNK) == (lane // DN_CHUNK)
    prefix_m = jnp.where(same_chunk & (lane <= row), 1.0, 0.0)
    suffix_m = jnp.where(same_chunk & (lane >= row), 1.0, 0.0)
    sum_m = jnp.concatenate([prefix_m, suffix_m], axis=0).astype(BF16)
    neg_a = -jnp.exp(alog_ref[...])
    dtb = dtb_ref[...]

    def tile(t, carry):
        rows = pl.ds(pl.multiple_of(t * LANES, LANES), LANES)
        x = x_ref[0, rows, :]
        beta = _sigmoid(x)
        z = x + dtb
        g = neg_a * (jnp.maximum(z, 0.0) + jnp.log1p(jnp.exp(-jnp.abs(z))))
        g_hi = g.astype(BF16)
        r1 = g - g_hi.astype(F32)
        g_mid = r1.astype(BF16)
        g_lo = (r1 - g_mid.astype(F32)).astype(BF16)
        sums = jnp.dot(sum_m, jnp.concatenate([g_hi, g_mid, g_lo], axis=1), preferred_element_type=F32)
        sums = sums[:, :LANES] + (sums[:, LANES:2 * LANES] + sums[:, 2 * LANES:])
        pre, suf = sums[:LANES], sums[LANES:]
        out = jnp.where(lane < 2 * N_HEADS, beta, jnp.where(lane < 3 * N_HEADS, pre, suf))
        o_ref[0, :, rows] = out.T[:4 * N_HEADS]
        return carry

    lax.fori_loop(0, l // LANES, tile, 0)


def _dn_gates(ba, alog_lanes, dtb_lanes):
    b, l, _ = ba.shape
    return pl.pallas_call(
        _dn_gates_kernel,
        grid=(b,),
        in_specs=[pl.BlockSpec((1, l, LANES), lambda i: (i, 0, 0)),
                  pl.BlockSpec((1, LANES), lambda i: (0, 0)),
                  pl.BlockSpec((1, LANES), lambda i: (0, 0))],
        out_specs=pl.BlockSpec((1, 4 * N_HEADS, l), lambda i: (i, 0, 0)),
        out_shape=jax.ShapeDtypeStruct((b, 4 * N_HEADS, l), F32),
        compiler_params=_cparams("parallel"),
        name="dn_gates",
    )(ba, alog_lanes, dtb_lanes)


def _dn_masks():
    row = lax.broadcasted_iota(jnp.int32, (LANES, LANES), 0)
    col = lax.broadcasted_iota(jnp.int32, (LANES, LANES), 1)
    fwd = row < DN_CHUNK
    same = (row // DN_CHUNK) == (col // DN_CHUNK)
    tril = same & ((fwd & (row >= col)) | (~fwd & (row <= col)))
    strict = tril & (row != col)
    return row, col, tril, strict


def _dn_prepare(h, n_rows, raw_refs, conv_refs, bg_ref, pad_ref, qkv_s, gate_s, defer=False):
    gate_s[...] = jnp.zeros(gate_s.shape, F32)
    for d in range(2):
        gate_s[d, 0:1, :] = bg_ref[0, pl.ds(d * N_HEADS + h, 1), :]
        gate_s[d, 1:2, :] = bg_ref[0, pl.ds((2 + d) * N_HEADS + h, 1), :]
    n_tiles = n_rows // LANES
    zeros_halo = jnp.zeros((CONV_HALO, LANES), F32)
    for which in range(3):
        pad_ref[which, 0:CONV_HALO, :] = zeros_halo
        pad_ref[which, CONV_HALO + n_rows:2 * CONV_HALO + n_rows, :] = zeros_halo
        pad_ref[which, CONV_HALO:CONV_HALO + n_rows, :] = raw_refs[which][0, 0].astype(F32)
    taps = [conv_refs[which][0] for which in range(3)]

    def tile(t, carry=None):
        rows = pl.ds(pl.multiple_of(t * LANES, LANES), LANES)
        base = t * LANES + CONV_HALO - CONV_K // 2
        ys = []
        for which in range(3):
            acc = pad_ref[which, pl.ds(base, LANES), :] * taps[which][0:1, :]
            for j in range(1, CONV_K):
                acc = acc + pad_ref[which, pl.ds(base + j, LANES), :] * taps[which][j:j + 1, :]
            ys.append(_silu(acc))
        q, k, v = ys
        qkv_s[0, rows, :] = q * (lax.rsqrt(jnp.sum(q * q, axis=-1, keepdims=True) + EPS) * (HEAD_DIM ** -0.5))
        qkv_s[1, rows, :] = k * lax.rsqrt(jnp.sum(k * k, axis=-1, keepdims=True) + EPS)
        qkv_s[2, rows, :] = v
        return carry

    if defer:
        return [functools.partial(tile, t) for t in range(n_tiles)]
    lax.fori_loop(0, n_tiles, tile, 0, unroll=2)
    return []


def _mm(x, y):
    return jnp.dot(x.astype(BF16), y.astype(BF16), preferred_element_type=F32)


def _unit_tri_solve(a_all, rhs_all, row, col, tick):
    blk = lambda m: (row // m) == (col // m)
    eye = jnp.where(row == col, 1.0, 0.0)
    ad = [jnp.where(blk(DN_SOLVE_BASE), a, 0.0) for a in a_all]
    t = [eye - x for x in ad]
    p = [_mm(x, x) for x in ad]
    tick()
    for _ in range(int(math.log2(DN_SOLVE_BASE)) - 2):
        tp = [_mm(jnp.concatenate([ti, pi], axis=0), pi) for ti, pi in zip(t, p)]
        tick()
        t = [ti + x[:LANES] for ti, x in zip(t, tp)]
        p = [x[LANES:] for x in tp]
    tp = [_mm(ti, pi) for ti, pi in zip(t, p)]
    tick()
    t = [ti + x for ti, x in zip(t, tp)]
    k = 2 * DN_SOLVE_BASE
    while k < DN_CHUNK:
        nk = [_mm(ti, jnp.where(blk(k) & ~blk(k // 2), a, 0.0)) for ti, a in zip(t, a_all)]
        tick()
        tn = [_mm(x, ti) for x, ti in zip(nk, t)]
        tick()
        t = [ti - x for ti, x in zip(t, tn)]
        k *= 2
    ny = [_mm(ti, jnp.concatenate([jnp.where(blk(k // 2), 0.0, a), r], axis=1))
          for ti, a, r in zip(t, a_all, rhs_all)]
    tick()
    ny2 = [_mm(x[:, :LANES], x[:, LANES:]) for x in ny]
    tick()
    return [x[:, LANES:] - z for x, z in zip(ny, ny2)]


def _dn_intra(steps, n, qkv_s, gate_s, lhs_s, c_s, o0_s, eg_s, with_out, slot0=0, tick=lambda: None):
    row, col, tril, strict = _dn_masks()
    is_f = row < DN_CHUNK
    nt = (((1,), (1,)), ((), ()))

    def load(i):
        ri = pl.ds(pl.multiple_of(i * DN_CHUNK, DN_CHUNK), DN_CHUNK)
        rj = pl.ds(pl.multiple_of((n - 1 - i) * DN_CHUNK, DN_CHUNK), DN_CHUNK)
        pair = lambda ref, a, b: jnp.concatenate([ref[a, ri, :], ref[b, rj, :]], axis=0)
        q2, k2, v2 = pair(qkv_s, 0, 0), pair(qkv_s, 1, 1), pair(qkv_s, 2, 2)
        j = n - 1 - i
        ga = gate_s[0, :, pl.ds(pl.multiple_of((i // 2) * LANES, LANES), LANES)]
        gb = gate_s[1, :, pl.ds(pl.multiple_of((j // 2) * LANES, LANES), LANES)]
        lo = lax.broadcasted_iota(jnp.int32, ga.shape, 1) < DN_CHUNK
        gates = jnp.where(i % 2 == 0, jnp.where(lo, ga, gb),
                          pltpu.roll(jnp.where(lo, gb, ga), DN_CHUNK, axis=1))
        g_row = jnp.broadcast_to(gates[1:2, :], (LANES, LANES))
        g2 = g_row.T
        b2 = jnp.broadcast_to(gates[0:1, :], (LANES, LANES)).T
        decay = jnp.exp(jnp.where(tril, g2 - g_row, -jnp.inf))
        return dict(q2=q2, k2=k2, v2=v2, b2=b2, g2=g2, decay=decay, kb2=k2 * b2, eg=jnp.exp(g2))

    st = [load(i) for i in steps]
    if with_out:
        kk = [lax.dot_general(jnp.concatenate([d["kb2"], d["q2"]], axis=0).astype(BF16), d["k2"].astype(BF16), nt,
                              preferred_element_type=F32) for d in st]
        a_qk = [jnp.where(tril, x[LANES:] * d["decay"], 0.0) for x, d in zip(kk, st)]
    else:
        kk = [lax.dot_general(d["kb2"].astype(BF16), d["k2"].astype(BF16), nt, preferred_element_type=F32)
              for d in st]
    tick()
    a = [jnp.where(strict, x[:LANES] * d["decay"], 0.0) for x, d in zip(kk, st)]
    sol = _unit_tri_solve(a, [jnp.concatenate([d["v2"] * d["b2"], d["kb2"] * d["eg"]], axis=1) for d in st],
                          row, col, tick)
    split = lambda x: [jnp.where(is_f, x, 0.0), jnp.where(is_f, 0.0, x)]
    kwu = []
    for d, x in zip(st, sol):
        g2 = d["g2"]
        gl_f, gl_b = g2[DN_CHUNK - 1:DN_CHUNK, :], g2[DN_CHUNK:DN_CHUNK + 1, :]
        gl = jnp.concatenate([jnp.broadcast_to(gl_f, (DN_CHUNK, LANES)),
                              jnp.broadcast_to(gl_b, (DN_CHUNK, LANES))], axis=0)
        kd2 = d["k2"] * jnp.exp(gl - g2)
        d["egl"] = jnp.exp(jnp.concatenate([gl_f, gl_b, jnp.zeros((6, LANES), F32)], axis=0))
        kwu.append(_mm(kd2.T, jnp.concatenate(split(x[:, LANES:]) + split(x[:, :LANES]), axis=1)))
    if with_out:
        awu = [_mm(x, y) for x, y in zip(a_qk, sol)]
    tick()
    for k, i in enumerate(steps):
        i = slot0 + i
        c_s[i, 0] = kwu[k][:, 2 * LANES:3 * LANES]
        c_s[i, 1] = kwu[k][:, 3 * LANES:]
        eg_s[i] = st[k]["egl"]
        lhs_s[i, 0, 0:LANES, :] = (-kwu[k][:, :LANES]).astype(BF16)
        lhs_s[i, 1, 0:LANES, :] = (-kwu[k][:, LANES:2 * LANES]).astype(BF16)
        if with_out:
            q_eff = st[k]["q2"] * st[k]["eg"] - awu[k][:, LANES:]
            o0_s[i] = awu[k][:, :LANES]
            lhs_s[i, 0, LANES:LANES + DN_CHUNK, :] = q_eff[:DN_CHUNK].astype(BF16)
            lhs_s[i, 1, LANES:LANES + DN_CHUNK, :] = q_eff[DN_CHUNK:].astype(BF16)


def _dn_recur(i, n, s_f, s_b, lhs_s, c_s, o0_s, eg_s, o_s, with_out, slot0=0):
    m = LANES + DN_CHUNK if with_out else LANES
    k = slot0 + i
    r_f = jnp.dot(lhs_s[k, 0, 0:m, :], s_f.astype(BF16), preferred_element_type=F32)
    r_b = jnp.dot(lhs_s[k, 1, 0:m, :], s_b.astype(BF16), preferred_element_type=F32)
    if with_out:
        o0 = o0_s[i]
        o_s[0, pl.ds(pl.multiple_of(i * DN_CHUNK, DN_CHUNK), DN_CHUNK), :] = r_f[LANES:] + o0[:DN_CHUNK]
        o_s[1, pl.ds(pl.multiple_of((n - 1 - i) * DN_CHUNK, DN_CHUNK), DN_CHUNK), :] = r_b[LANES:] + o0[DN_CHUNK:]
    eg = eg_s[k]
    s_f = s_f * jnp.broadcast_to(eg[0:1, :], (LANES, LANES)) + r_f[:LANES] + c_s[k, 0]
    s_b = s_b * jnp.broadcast_to(eg[1:2, :], (LANES, LANES)) + r_b[:LANES] + c_s[k, 1]
    return s_f, s_b


def _delta_kernel(q_ref, k_ref, v_ref, z_ref, cq_ref, ck_ref, cv_ref, bg_ref, cbg_ref,
                  wq_ref, wk_ref, wv_ref, nw_ref, o_ref,
                  pad_s, qkv_s, gate_s, cqkv_s, cgate_s, lhs_s, c_s, o0_s, eg_s, o_s):
    h = pl.program_id(1)
    l = q_ref.shape[2]
    lc = cq_ref.shape[2]
    n, nc = l // DN_CHUNK, lc // DN_CHUNK
    conv_refs = (wq_ref, wk_ref, wv_ref)
    stage = (lhs_s, c_s, o0_s, eg_s)

    _dn_prepare(h, lc, (cq_ref, ck_ref, cv_ref), conv_refs, cbg_ref, pad_s, cqkv_s, cgate_s)
    lat_tiles = _dn_prepare(h, l, (q_ref, k_ref, v_ref), conv_refs, bg_ref, pad_s, qkv_s, gate_s, defer=True)

    group = math.gcd(DN_GROUP, n)
    n_groups = n // group
    gc = math.gcd(DN_GROUP, nc)

    def two_tiles():
        for _ in range(2):
            if lat_tiles:
                lat_tiles.pop(0)()

    for g in range(nc // gc):
        _dn_intra([g * gc + k for k in range(gc)], nc, cqkv_s, cgate_s, *stage, with_out=False, slot0=n,
                  tick=two_tiles)
    while lat_tiles:
        two_tiles()
    state = [jnp.zeros((LANES, LANES), F32)] * 2

    def ctx_step(i):
        state[:] = _dn_recur(i, nc, state[0], state[1], *stage, o_s, with_out=False, slot0=n)

    def lat_step(i):
        state[:] = _dn_recur(i, n, state[0], state[1], *stage, o_s, with_out=True)

    def lat_group(g, scan_steps):
        pending = list(scan_steps)
        tick = lambda: pending.pop(0)() if pending else None
        _dn_intra([g * group + k for k in range(group)], n, qkv_s, gate_s, *stage, with_out=True, tick=tick)
        while pending:
            tick()

    lat_group(0, [functools.partial(ctx_step, i) for i in range(nc)])

    def pipelined(g, s):
        state[:] = s
        lat_group(g, [functools.partial(lat_step, (g - 1) * group + k) for k in range(group)])
        return tuple(state)

    state[:] = lax.fori_loop(1, n_groups, pipelined, tuple(state))

    nw = nw_ref[...]

    def out_tile(t):
        rows = pl.ds(t * LANES, LANES)
        o = o_s[0, rows, :] + o_s[1, rows, :]
        y = o * lax.rsqrt(jnp.mean(o * o, axis=-1, keepdims=True) + EPS) * nw
        o_ref[0, 0, rows, :] = (y * _silu(z_ref[0, 0, rows, :].astype(F32))).astype(o_ref.dtype)

    per_tile = LANES // DN_CHUNK
    final_at = lambda t: max(per_tile * t + per_tile - 1, n - 1 - per_tile * t)
    first = (n_groups - 1) * group
    ready = [t for t in range(l // LANES) if final_at(t) < first]
    for i in range(first, n):
        lat_step(i)
        ready += [t for t in range(l // LANES) if final_at(t) == i]
        if ready:
            out_tile(ready.pop(0))
    for t in ready:
        out_tile(t)


def _delta_branch(p_lat, p_ctx, bg, cbg, conv_taps, norm_w):
    b, _, l, _ = p_lat.shape
    lc = p_ctx.shape[2]
    n, nc = l // DN_CHUNK, lc // DN_CHUNK
    hb =lambda off: pl.BlockSpec((1, 1, l, LANES), lambda i, j, off=off: (i, off + j, 0, 0))
    cb = lambda off: pl.BlockSpec((1, 1, lc, LANES), lambda i, j, off=off: (i, off + j, 0, 0))
    tb = lambda off: pl.BlockSpec((1, 8, LANES), lambda i, j, off=off: (off + j, 0, 0))
    return pl.pallas_call(
        _delta_kernel,
        grid=(b, N_HEADS),
        in_specs=[hb(0), hb(N_HEADS), hb(2 * N_HEADS), hb(3 * N_HEADS),
                  cb(0), cb(N_HEADS), cb(2 * N_HEADS),
                  pl.BlockSpec((1, 4 * N_HEADS, l), lambda i, j: (i, 0, 0)),
                  pl.BlockSpec((1, 4 * N_HEADS, lc), lambda i, j: (i, 0, 0)),
                  tb(0), tb(N_HEADS), tb(2 * N_HEADS),
                  pl.BlockSpec((1, LANES), lambda i, j: (0, 0))],
        out_specs=pl.BlockSpec((1, 1, l, LANES), lambda i, j: (i, j, 0, 0)),
        out_shape=jax.ShapeDtypeStruct((b, N_HEADS, l, LANES), BF16),
        scratch_shapes=[pltpu.VMEM((3, l + 2 * CONV_HALO, LANES), F32),
                        pltpu.VMEM((3, l, LANES), F32),
                        pltpu.VMEM((2, 8, l), F32),
                        pltpu.VMEM((3, lc, LANES), F32),
                        pltpu.VMEM((2, 8, lc), F32),
                        pltpu.VMEM((n + nc, 2, LANES + DN_CHUNK, LANES), BF16),
                        pltpu.VMEM((n + nc, 2, LANES, LANES), F32),
                        pltpu.VMEM((n, LANES, LANES), F32),
                        pltpu.VMEM((n + nc, 8, LANES), F32),
                        pltpu.VMEM((2, l, LANES), F32)],
        compiler_params=_cparams("parallel", "arbitrary"),
        name="delta_branch",
    )(p_lat, p_lat, p_lat, p_lat, p_ctx, p_ctx, p_ctx, bg, cbg,
      conv_taps, conv_taps, conv_taps, norm_w.reshape(1, LANES))


def _rope_perm():
    quarter = HEAD_DIM // 4
    return np.concatenate([np.arange(quarter), 2 * quarter + np.arange(quarter),
                           quarter + np.arange(quarter), 3 * quarter + np.arange(quarter)])


def _rope(x, cos, sin):
    return x * cos + pltpu.roll(x, HEAD_DIM // 2, axis=1) * sin


def _head_rms(x, w):
    return x * lax.rsqrt(jnp.mean(x * x, axis=-1, keepdims=True) + EPS) * w


def _na_kernel(q_ref, k_ref, v_ref, ck_ref, cv_ref, qnw_ref, knw_ref, cos_ref, sin_ref, bias_ref, o_ref,
               q_s, k_s, ck_s):
    l = q_ref.shape[2]
    rows = l // GRID_W
    kh = min(NA_KH, rows)
    n_loc = kh * GRID_W
    qnw, knw = qnw_ref[...], knw_ref[...]

    def prep(t):
        r = pl.ds(t * LANES, LANES)
        cos, sin = cos_ref[r, :], sin_ref[r, :]
        q = _rope(_head_rms(q_ref[0, 0, r, :].astype(F32), qnw), cos, sin) * (HEAD_DIM ** -0.5)
        q_s[r, :] = q.astype(BF16)
        k_s[r, :] = _rope(_head_rms(k_ref[0, 0, r, :].astype(F32), knw), cos, sin).astype(BF16)

    ck_s[...] = _head_rms(ck_ref[0, 0].astype(F32), knw).astype(BF16)
    nt = (((1,), (1,)), ((), ()))
    group = math.gcd(NA_ROW_GROUP, rows)
    window_start = lambda r: min(max(r - kh // 2, 0), rows - kh)
    tiles_needed = lambda g: -(-(window_start(g * group + group - 1) + kh) * GRID_W // LANES)

    def row_group(g, pending):
        tick = lambda: pending.pop(0)() if pending else None
        rr = [g * group + k for k in range(group)]
        rs = [window_start(r) for r in rr]
        qr = [q_s[pl.ds(r * GRID_W, GRID_W), :] for r in rr]
        kloc = [pl.ds(x * GRID_W, n_loc) for x in rs]
        s_loc = [lax.dot_general(q, k_s[kl, :], nt, preferred_element_type=F32) for q, kl in zip(qr, kloc)]
        tick()
        s_ctx = [lax.dot_general(q, ck_s[...], nt, preferred_element_type=F32) for q in qr]
        tick()
        s_loc = [s + bias_ref[0, r - x] for s, r, x in zip(s_loc, rr, rs)]
        m = [jnp.maximum(jnp.max(sl, axis=-1, keepdims=True), jnp.max(sc, axis=-1, keepdims=True))
             for sl, sc in zip(s_loc, s_ctx)]
        tick()
        p_loc = [jnp.exp(s - mi) for s, mi in zip(s_loc, m)]
        p_ctx = [jnp.exp(s - mi) for s, mi in zip(s_ctx, m)]
        tick()
        denom = [jnp.sum(pl_, axis=-1, keepdims=True) + jnp.sum(pc, axis=-1, keepdims=True)
                 for pl_, pc in zip(p_loc, p_ctx)]
        o_loc = [jnp.dot(p.astype(BF16), v_ref[0, 0, kl, :], preferred_element_type=F32)
                 for p, kl in zip(p_loc, kloc)]
        o_ctx = [jnp.dot(p.astype(BF16), cv_ref[0, 0], preferred_element_type=F32) for p in p_ctx]
        while pending:
            tick()
        for r, ol, oc, d in zip(rr, o_loc, o_ctx, denom):
            o_ref[0, 0, pl.ds(r * GRID_W, GRID_W), :] = ((ol + oc) / d).astype(o_ref.dtype)

    n_groups = rows // group
    for t in range(tiles_needed(0)):
        prep(t)
    for g in range(n_groups):
        nxt = range(tiles_needed(g), tiles_needed(g + 1)) if g + 1 < n_groups else ()
        row_group(g, [functools.partial(prep, t) for t in nxt])


def _na_tables(l):
    pos = jnp.arange(l)
    row = (pos // GRID_W).astype(F32)
    col = (pos % GRID_W).astype(F32)
    half = HEAD_DIM // 2
    inv_freq = ROPE_THETA ** (-jnp.arange(0, half, 2, dtype=F32) / half)
    ang_r = row[:, None] * inv_freq[None, :]
    ang_c = col[:, None] * inv_freq[None, :]
    cos = jnp.concatenate([jnp.cos(ang_r), jnp.cos(ang_r), jnp.cos(ang_c), jnp.cos(ang_c)], axis=-1)
    sin = jnp.concatenate([-jnp.sin(ang_r), jnp.sin(ang_r), -jnp.sin(ang_c), jnp.sin(ang_c)], axis=-1)
    perm = _rope_perm()
    return cos[:, perm], sin[:, perm]


def _na_bias_table(rpb, rows):
    kh = min(NA_KH, rows)
    t = np.arange(NA_KH)[:, None]
    i = np.arange(kh)[None, :]
    dr = np.clip(i - t + NA_KH - 1, 0, 2 * NA_KH - 2)
    q = np.arange(GRID_W)[:, None]
    kc = np.arange(GRID_W)[None, :]
    qstart = np.clip(q - NA_KW // 2, 0, GRID_W - NA_KW)
    in_win = (kc >= qstart) & (kc < qstart + NA_KW)
    dc = np.clip(kc - q + NA_KW - 1, 0, 2 * NA_KW - 2)
    pick_r = (dr[:, :, None] == np.arange(2 * NA_KH - 1)).astype(np.float32)
    pick_c = (np.arange(2 * NA_KW - 1)[:, None, None] == dc[None]).astype(np.float32)
    tab = jnp.einsum("tir,hrc,cqk->htqik", pick_r, rpb.astype(F32), pick_c, precision=lax.Precision.HIGHEST)
    tab = jnp.where(jnp.asarray(in_win)[None, None, :, None, :], tab, NEG_INF)
    return tab.reshape(rpb.shape[0], NA_KH, GRID_W, kh * GRID_W)


def _na_branch(p_lat, p_ctx, q_norm_w, k_norm_w, rpb):
    b, _, l, _ = p_lat.shape
    lc = p_ctx.shape[2]
    rows = l // GRID_W
    n_loc = min(NA_KH, rows) * GRID_W
    cos, sin = _na_tables(l)
    bias = _na_bias_table(rpb, rows)
    hb = lambda off: pl.BlockSpec((1, 1, l, LANES), lambda j, i, off=off: (i, off + j, 0, 0))
    cb = lambda off: pl.BlockSpec((1, 1, lc, LANES), lambda j, i, off=off: (i, off + j, 0, 0))
    const = lambda shape: pl.BlockSpec(shape, lambda j, i: (0,) * len(shape))
    return pl.pallas_call(
        _na_kernel,
        grid=(N_HEADS, b),
        in_specs=[hb(4 * N_HEADS), hb(5 * N_HEADS), hb(6 * N_HEADS), cb(3 * N_HEADS), cb(4 * N_HEADS),
                  const((1, LANES)), const((1, LANES)), const((l, LANES)), const((l, LANES)),
                  pl.BlockSpec((1, NA_KH, GRID_W, n_loc), lambda j, i: (j, 0, 0, 0))],
        out_specs=pl.BlockSpec((1, 1, l, LANES), lambda j, i: (i, j, 0, 0)),
        out_shape=jax.ShapeDtypeStruct((b, N_HEADS, l, LANES), BF16),
        scratch_shapes=[pltpu.VMEM((l, LANES), BF16), pltpu.VMEM((l, LANES), BF16),
                        pltpu.VMEM((lc, LANES), BF16)],
        compiler_params=_cparams("parallel", "arbitrary"),
        name="na_branch",
    )(p_lat, p_lat, p_lat, p_ctx, p_ctx, q_norm_w[_rope_perm()].reshape(1, LANES),
      k_norm_w[_rope_perm()].reshape(1, LANES), cos, sin, bias)


def _merge_kernel(a_ref, b_ref, wa_ref, wb_ref, ga_ref, gb_ref, o_ref, a_s, b_s):
    @pl.when(pl.program_id(1) == 0)
    def _():
        for k in range(N_HEADS):
            a_s[:, k * LANES:(k + 1) * LANES] = a_ref[0, k]
            b_s[:, k * LANES:(k + 1) * LANES] = b_ref[0, k]

    ya = jnp.dot(a_s[...], wa_ref[...], preferred_element_type=F32)
    yb = jnp.dot(b_s[...], wb_ref[...], preferred_element_type=F32)
    for k in range(ga_ref.shape[1]):
        cols = slice(k * LANES, (k + 1) * LANES)
        o_ref[:, cols] = (_sigmoid(ga_ref[0, k].astype(F32)) * ya[:, cols]
                          + _sigmoid(gb_ref[0, k].astype(F32)) * yb[:, cols]).astype(o_ref.dtype)


def _merge(dn_o, na_o, w_a, w_b, p_lat, tm=1024, tn=512):
    b, _, l, _ = dn_o.shape
    d = w_a.shape[1]
    tpb = l // tm
    nb = tn // LANES
    head_blk = pl.BlockSpec((1, N_HEADS, tm, LANES), lambda i, j: (i // tpb, 0, i % tpb, 0))
    gate_blk = lambda off: pl.BlockSpec((1, nb, tm, LANES),
                                        lambda i, j, off=off: (i // tpb, off // nb + j, i % tpb, 0))
    w_blk = pl.BlockSpec((N_HEADS * LANES, tn), lambda i, j: (0, j))
    return pl.pallas_call(
        _merge_kernel,
        grid=(b * tpb, d // tn),
        in_specs=[head_blk, head_blk, w_blk, w_blk, gate_blk(7 * N_HEADS), gate_blk(8 * N_HEADS)],
        out_specs=pl.BlockSpec((tm, tn), lambda i, j: (i, j)),
        out_shape=jax.ShapeDtypeStruct((b * l, d), BF16),
        scratch_shapes=[pltpu.VMEM((tm, N_HEADS * LANES), BF16), pltpu.VMEM((tm, N_HEADS * LANES), BF16)],
        compiler_params=_cparams("parallel", "arbitrary"),
        name="merge",
    )(dn_o, na_o, w_a, w_b, p_lat, p_lat)


def _out_kernel(y_ref, w_ref, x_ref, g1_ref, nw_ref, sc_ref, sh_ref, wr_hi_ref, wr_lo_ref, x1_ref, h2_ref, lg_ref):
    x1 = x_ref[...] + g1_ref[0] * jnp.dot(y_ref[...], w_ref[...], preferred_element_type=F32)
    x1_ref[...] = x1
    h2 = (x1 * lax.rsqrt(jnp.mean(x1 * x1, axis=-1, keepdims=True) + EPS) * nw_ref[...]
          * (1.0 + sc_ref[0]) + sh_ref[0])
    h_hi = h2.astype(BF16)
    h2_ref[...] = h_hi
    h_lo = (h2 - h_hi.astype(F32)).astype(BF16)
    lg_ref[...] = (jnp.dot(h_hi, wr_hi_ref[...], preferred_element_type=F32)
                   + (jnp.dot(h_hi, wr_lo_ref[...], preferred_element_type=F32)
                      + jnp.dot(h_lo, wr_hi_ref[...], preferred_element_type=F32)))


def _out_proj(y, w_out, x2d, g1, norm_w, scale, shift, w_router_pad, l, tm=256):
    m, d = x2d.shape
    tpb = l // tm
    wr_hi = w_router_pad.astype(BF16)
    wr_lo = (w_router_pad - wr_hi.astype(F32)).astype(BF16)
    row_blk = lambda: pl.BlockSpec((tm, d), lambda i: (i, 0))
    mod_blk = lambda: pl.BlockSpec((1, 1, d), lambda i: (i // tpb, 0, 0))
    return pl.pallas_call(
        _out_kernel,
        grid=(m // tm,),
        in_specs=[row_blk(), pl.BlockSpec((d, d), lambda i: (0, 0)), row_blk(), mod_blk(),
                  pl.BlockSpec((1, d), lambda i: (0, 0)), mod_blk(), mod_blk(),
                  pl.BlockSpec((d, LANES), lambda i: (0, 0)), pl.BlockSpec((d, LANES), lambda i: (0, 0))],
        out_specs=[row_blk(), row_blk(), pl.BlockSpec((tm, LANES), lambda i: (i, 0))],
        out_shape=[jax.ShapeDtypeStruct((m, d), F32), jax.ShapeDtypeStruct((m, d), BF16),
                   jax.ShapeDtypeStruct((m, LANES), F32)],
        compiler_params=_cparams("parallel"),
        name="out_proj",
    )(y, w_out, x2d, g1, norm_w.reshape(1, d), scale, shift, wr_hi, wr_lo)


def _route_kernel(lg_ref, slot_t_ref, gate_t_ref, slot_ref, aff_s, slot_s):
    l = lg_ref.shape[0]
    n_tiles = l // LANES
    cap = EC_CAPACITY_FACTOR * l // N_EXPERTS
    lane = lax.broadcasted_iota(jnp.int32, (LANES, LANES), 1)
    row = lax.broadcasted_iota(jnp.int32, (LANES, LANES), 0)

    for t in range(n_tiles):
        x = jnp.where(lane < N_EXPERTS, lg_ref[t * LANES:(t + 1) * LANES, :], -jnp.inf)
        e = jnp.exp(x - jnp.max(x, axis=-1, keepdims=True))
        aff = e / jnp.sum(e, axis=-1, keepdims=True)
        aff_s[:, t * LANES:(t + 1) * LANES] = aff.T[:N_EXPERTS]

    aff_t = aff_s[...]
    keys = pltpu.bitcast(aff_t, jnp.int32)
    count_ge = lambda thr: jnp.sum(jnp.where(keys >= thr, 1.0, 0.0), axis=1, keepdims=True)

    def bisect(_, c):
        lo, hi = c
        mid = lo + (hi - lo) // 2
        ok = count_ge(mid) >= cap
        return jnp.where(ok, mid, lo), jnp.where(ok, hi, mid)

    inf_bits = 0x7F800000
    thr, _ = lax.fori_loop(0, 31, bisect, (jnp.zeros((N_EXPERTS, 1), jnp.int32),
                                           jnp.full((N_EXPERTS, 1), inf_bits, jnp.int32)))
    gt, eq = keys > thr, keys == thr
    need = cap - jnp.sum(jnp.where(gt, 1.0, 0.0), axis=1, keepdims=True)
    before = jnp.where(row < lane, 1.0, 0.0).astype(BF16)

    def excl_prefix(flags):
        out, off = [], jnp.zeros((N_EXPERTS, 1), F32)
        ones = jnp.where(flags, 1.0, 0.0)
        for t in range(n_tiles):
            f = ones[:, t * LANES:(t + 1) * LANES]
            out.append(jnp.dot(f.astype(BF16), before, preferred_element_type=F32) + off)
            off = off + jnp.sum(f, axis=1, keepdims=True)
        return jnp.concatenate(out, axis=1)

    sel = gt | (eq & (excl_prefix(eq) < need))
    slot_f = jnp.where(sel, excl_prefix(sel), -1.0)
    slot_t_ref[0] = slot_f.astype(jnp.int32)
    gate_t_ref[0] = aff_t
    slot_s[...] = jnp.full(slot_s.shape, -1.0, F32)
    slot_s[0:N_EXPERTS, :] = slot_f
    for t in range(n_tiles):
        slot_ref[0, t * LANES:(t + 1) * LANES, :] = slot_s[:, t * LANES:(t + 1) * LANES].T.astype(jnp.int32)


def _route(logits, b, l):
    return pl.pallas_call(
        _route_kernel,
        grid=(b,),
        in_specs=[pl.BlockSpec((l, LANES), lambda i: (i, 0))],
        out_specs=[pl.BlockSpec((1, N_EXPERTS, l), lambda i: (i, 0, 0)),
                   pl.BlockSpec((1, N_EXPERTS, l), lambda i: (i, 0, 0)),
                   pl.BlockSpec((1, l, LANES), lambda i: (i, 0, 0))],
        out_shape=[jax.ShapeDtypeStruct((b, N_EXPERTS, l), jnp.int32),
                   jax.ShapeDtypeStruct((b, N_EXPERTS, l), F32),
                   jax.ShapeDtypeStruct((b, l, LANES), jnp.int32)],
        scratch_shapes=[pltpu.VMEM((N_EXPERTS, l), F32), pltpu.VMEM((LANES, l), F32)],
        compiler_params=_cparams("parallel"),
        name="route",
    )(logits)


def _ffn_kernel(h_ref, slot_t_ref, gate_t_ref, w1_ref, w3_ref, w2_ref, o_ref):
    e = pl.program_id(0)
    cap, l = o_ref.shape[2], h_ref.shape[1]
    hit = lax.broadcasted_iota(jnp.int32, (cap, l), 0) == slot_t_ref[0, pl.ds(e, 1), :]
    gate = jnp.sum(jnp.where(hit, gate_t_ref[0, pl.ds(e, 1), :], 0.0), axis=1, keepdims=True)
    x = jnp.dot(jnp.where(hit, 1.0, 0.0).astype(BF16), h_ref[0], preferred_element_type=F32).astype(BF16)
    h1 = jnp.dot(x, w1_ref[0], preferred_element_type=F32)
    h3 = jnp.dot(x, w3_ref[0], preferred_element_type=F32)
    hid = (_silu(h1) * h3).astype(BF16)
    o_ref[0, 0] = (jnp.dot(hid, w2_ref[0], preferred_element_type=F32) * gate).astype(o_ref.dtype)


def _expert_ffn(h2, slot_t, gate_t, w1, w3, w2, cap):
    b, l, d = h2.shape
    e, _, f = w1.shape
    return pl.pallas_call(
        _ffn_kernel,
        grid=(e, b),
        in_specs=[pl.BlockSpec((1, l, d), lambda j, i: (i, 0, 0)),
                  pl.BlockSpec((1, e, l), lambda j, i: (i, 0, 0)),
                  pl.BlockSpec((1, e, l), lambda j, i: (i, 0, 0)),
                  pl.BlockSpec((1, d, f), lambda j, i: (j, 0, 0)),
                  pl.BlockSpec((1, d, f), lambda j, i: (j, 0, 0)),
                  pl.BlockSpec((1, f, d), lambda j, i: (j, 0, 0))],
        out_specs=pl.BlockSpec((1, 1, cap, d), lambda j, i: (i, j, 0, 0)),
        out_shape=jax.ShapeDtypeStruct((b, e, cap, d), BF16),
        compiler_params=_cparams("parallel", "arbitrary"),
        name="expert_ffn",
    )(h2, slot_t, gate_t, w1, w3, w2)


def _combine_kernel(slot_ref, y_ref, x1_ref, g2_ref, o_ref):
    tm = slot_ref.shape[1]
    cap = y_ref.shape[1] // N_EXPERTS
    slot = slot_ref[0]
    j = lax.broadcasted_iota(jnp.int32, (tm, cap), 1)
    onehot = jnp.concatenate([jnp.where(slot[:, e:e + 1] == j, 1.0, 0.0).astype(BF16) for e in range(N_EXPERTS)],
                             axis=1)
    o_ref[0] = x1_ref[0] + g2_ref[0] * jnp.dot(onehot, y_ref[0], preferred_element_type=F32)


def _combine(slot, ye, x1, g2, tm=512, tn=1024):
    b, l, d = x1.shape
    ec = ye.shape[1]
    return pl.pallas_call(
        _combine_kernel,
        grid=(b, d // tn, l // tm),
        in_specs=[pl.BlockSpec((1, tm, LANES), lambda i, n, m: (i, m, 0)),
                  pl.BlockSpec((1, ec, tn), lambda i, n, m: (i, 0, n)),
                  pl.BlockSpec((1, tm, tn), lambda i, n, m: (i, m, n)),
                  pl.BlockSpec((1, 1, tn), lambda i, n, m: (i, 0, n))],
        out_specs=pl.BlockSpec((1, tm, tn), lambda i, n, m: (i, m, n)),
        out_shape=jax.ShapeDtypeStruct((b, l, d), F32),
        compiler_params=_cparams("parallel", "parallel", "arbitrary"),
        name="moe_combine",
    )(slot, ye, x1, g2)


def _layer(x, ctx, mod, mod_c, norm1_w, w_in, conv_w, a_log, dt_bias, dn_norm_w, q_norm_w, k_norm_w, rpb,
           w_a, w_b, w_out, norm2_w, w_router, w1, w3, w2):
    b, l, d = x.shape
    sh1, sc1, g1, sh2, sc2, g2 = [m[:, None, :] for m in jnp.split(mod, 6, axis=-1)]
    sh1c, sc1c = mod_c[None, None, :d], mod_c[None, None, d:2 * d]

    hd = N_HEADS * HEAD_DIM
    w_pack = _pack_w_in(w_in, 4 * hd, 4 * N_HEADS, 4 * hd, 6 * hd)
    n_main = 7 * hd + 2 * d
    tn = 1024
    p_lat, ba = _norm_proj(x, norm1_w, sc1, sh1, w_pack, 1024, tn, n_main // tn, lambda j: j, n_main)
    skip_from, skip = 3 * hd // tn, 2 * hd // tn
    p_ctx, cba = _norm_proj(ctx, norm1_w, sc1c, sh1c, w_pack, min(1024, b * ctx.shape[1]), tn, 5 * hd // tn,
                            lambda j: jnp.where(j < skip_from, j, j + skip), n_main)

    lanes_pad = lambda v: jnp.pad(v.reshape(1, -1).astype(F32), ((0, 0), (2 * N_HEADS, LANES - 4 * N_HEADS)))
    alog_l, dtb_l = lanes_pad(a_log), lanes_pad(dt_bias)
    bg, cbg = _dn_gates(ba, alog_l, dtb_l), _dn_gates(cba, alog_l, dtb_l)
    taps = jnp.pad(conv_w.astype(F32), ((0, 8 - CONV_K), (0, 0))).reshape(8, 3 * N_HEADS, LANES).transpose(1, 0, 2)
    dn_o = _delta_branch(p_lat, p_ctx, bg, cbg, taps, dn_norm_w.astype(F32))
    na_o = _na_branch(p_lat, p_ctx, q_norm_w.astype(F32), k_norm_w.astype(F32), rpb)

    y = _merge(dn_o, na_o, w_a.astype(BF16), w_b.astype(BF16), p_lat)
    w_router_pad = jnp.pad(w_router.astype(F32), ((0, 0), (0, LANES - N_EXPERTS)))
    x1, h2, logits = _out_proj(y, w_out.astype(BF16), x.reshape(b * l, d), g1, norm2_w, sc2, sh2,
                               w_router_pad, l)

    cap = EC_CAPACITY_FACTOR * l // N_EXPERTS
    slot_t, gate_t, slot = _route(logits, b, l)
    ye = _expert_ffn(h2.reshape(b, l, d), slot_t, gate_t, w1.astype(BF16), w3.astype(BF16), w2.astype(BF16), cap)
    return _combine(slot, ye.reshape(b, N_EXPERTS * cap, d), x1.reshape(b, l, d), g2)


def kernel(x, c, ctx, c_ctx, ada_w, ada_b, norm1_w, w_in, conv_w, dn_a_log, dn_dt_bias, dn_norm_w,
           na_q_norm_w, na_k_norm_w, na_rpb, w_branch_a, w_branch_b, w_out, norm2_w, w_router,
           expert_w1, expert_w3, expert_w2):
    b = x.shape[0]
    depth = ada_w.shape[0]
    cvec = jnp.concatenate([c, c_ctx[None, :], jnp.zeros((16 - b - 1, c.shape[1]), c.dtype)], axis=0)
    for i in range(depth):
        mod_all = _adaln_mod(cvec, ada_w[i], ada_b[i])
        x = _layer(x, ctx, mod_all[:b], mod_all[b], norm1_w[i], w_in[i], conv_w[i], dn_a_log[i],
                   dn_dt_bias[i], dn_norm_w[i], na_q_norm_w[i], na_k_norm_w[i], na_rpb[i],
                   w_branch_a[i], w_branch_b[i], w_out[i], norm2_w[i], w_router[i],
                   expert_w1[i], expert_w3[i], expert_w2[i])
    return x
```
